```python
import math
import jax
import jax.numpy as jnp
from jax import lax
import numpy as np

D_MODEL = 1024
BATCH = 8
SEQ = 2048
DEPTH = 2

CTX_LEN = 256
GRID_W = 64
EPS = 1e-6
ROPE_BASE = 10000.0

GDN_HEADS = 4
GDN_DK = 128
GDN_DV = 128
GDN_CHUNK = 64
SHORT_CONV = 3
MLA_HEADS = 4
MLA_Q_RANK = 384
MLA_KV_RANK = 256
MLA_NOPE = 128
MLA_ROPE = 64
MLA_DV = 128
MLA_SCALE = (MLA_NOPE + MLA_ROPE) ** -0.5
Q_BLOCK = 128
RET_HEADS = 4
RET_DK = 128
RET_DV = 128
RET_CHUNK = 64
RET_DECAY_BASE = 5.0
RET_DIR_OFFSET = 0.5
D_FF = 2816
FFN_CONV = 3
N_BRANCH = 3

GDN_QK = GDN_HEADS * GDN_DK
GDN_V = GDN_HEADS * GDN_DV
MLA_OUT = MLA_HEADS * MLA_DV
RET_QK = RET_HEADS * RET_DK
RET_V = RET_HEADS * RET_DV
IN_SPLITS = (2 * GDN_QK + GDN_V, GDN_V, 4 * GDN_HEADS, MLA_Q_RANK, MLA_KV_RANK, MLA_ROPE,
             2 * RET_QK + 2 * RET_V, N_BRANCH * D_MODEL)
IN_COLS = sum(IN_SPLITS)

kernel_name = 'hybrid_gdn_mla_retention_flow_block'


def rms_norm(x, w):
    xf = x.astype(jnp.float32)
    y = xf * lax.rsqrt(jnp.mean(xf * xf, axis=-1, keepdims=True) + EPS)
    return (y * w.astype(jnp.float32)).astype(x.dtype)


def modulate(h, shift, scale):
    return h * (1 + scale) + shift


def l2norm(t):
    tf = t.astype(jnp.float32)
    return tf * lax.rsqrt(jnp.sum(tf * tf, axis=-1, keepdims=True) + EPS)


def flip(t):
    return jnp.flip(t, axis=1)


def dwconv(x, w):
    K, C = w.shape
    return lax.conv_general_dilated(x, w[:, None, :].astype(x.dtype), window_strides=(1,),
                                    padding=[(K // 2, K // 2)],
                                    dimension_numbers=('NWC', 'WIO', 'NWC'),
                                    feature_group_count=C)


def axial_rope(n, d):
    rows = n // GRID_W
    r = jnp.repeat(jnp.arange(rows, dtype=jnp.float32), GRID_W)
    col = jnp.tile(jnp.arange(GRID_W, dtype=jnp.float32), rows)
    quarter = d // 4
    inv = ROPE_BASE ** (-jnp.arange(quarter, dtype=jnp.float32) / quarter)
    ang = jnp.concatenate([r[:, None] * inv, col[:, None] * inv], axis=-1)
    return jnp.cos(ang), jnp.sin(ang)


def apply_rope(t, cos, sin):
    half = t.shape[-1] // 2
    t1, t2 = t[..., :half], t[..., half:]
    cos = cos[None, :, None, :].astype(t.dtype)
    sin = sin[None, :, None, :].astype(t.dtype)
    return jnp.concatenate([t1 * cos - t2 * sin, t1 * sin + t2 * cos], axis=-1)


def to_chunks(t, C):
    B, N, H, d = t.shape
    return t.reshape(B, N // C, C, H, d).transpose(1, 0, 3, 2, 4)


def from_chunks(t):
    nc, B, H, C, d = t.shape
    return t.transpose(1, 0, 3, 2, 4).reshape(B, nc * C, H, d)


def gated_head_norm(o, z, w):
    B, N, H, d = o.shape
    y = rms_norm(o, w).reshape(B, N, H * d)
    return (y * jax.nn.silu(z.astype(jnp.float32))).astype(z.dtype)


def gdn_chunked(q, k, v, g, beta, s0):
    f32 = jnp.float32
    dv = v.shape[-1]
    C = GDN_CHUNK
    qc = to_chunks(q.astype(f32), C)
    kc = to_chunks(k.astype(f32), C)
    vc = to_chunks(v.astype(f32), C)
    gcum = jnp.cumsum(to_chunks(g.astype(f32)[..., None], C)[..., 0], axis=-1)
    bc = to_chunks(beta.astype(f32)[..., None], C)
    lower = jnp.tril(jnp.ones((C, C), dtype=bool))
    strict = jnp.tril(jnp.ones((C, C), dtype=bool), -1)
    decay = jnp.exp(jnp.where(lower, gcum[..., :, None] - gcum[..., None, :], -jnp.inf))
    kb = kc * bc
    a_mat = jnp.eye(C, dtype=f32) + jnp.where(strict, jnp.einsum('nbhid,nbhjd->nbhij', kb, kc) * decay, 0.0)
    rhs = jnp.concatenate([vc * bc, kb * jnp.exp(gcum)[..., None]], axis=-1)
    sol = lax.linalg.triangular_solve(a_mat, rhs, left_side=True, lower=True, unit_diagonal=True)
    u, w = sol[..., :dv], sol[..., dv:]
    attn = jnp.einsum('nbhid,nbhjd->nbhij', qc, kc) * decay
    q_dec = qc * jnp.exp(gcum)[..., None]
    k_dec = kc * jnp.exp(gcum[..., -1:] - gcum)[..., None]
    c_dec = jnp.exp(gcum[..., -1])[..., None, None]

    def step(S, xs):
        u_i, w_i, a_i, qd_i, kd_i, cd_i = xs
        v_new = u_i - w_i @ S
        o_i = qd_i @ S + a_i @ v_new
        S = cd_i * S + jnp.swapaxes(kd_i, -1, -2) @ v_new
        return S, o_i

    s_final, o = lax.scan(step, s0, (u, w, attn, q_dec, k_dec, c_dec))
    return from_chunks(o), s_final


def retention_chunked(q, k, v, log_gamma, s0):
    f32 = jnp.float32
    C = RET_CHUNK
    qc = to_chunks(q.astype(f32), C)
    kc = to_chunks(k.astype(f32), C)
    vc = to_chunks(v.astype(f32), C)
    lg = log_gamma.astype(f32)
    pos = jnp.arange(C, dtype=f32)
    rel = pos[:, None] - pos[None, :]
    decay = jnp.where(rel >= 0, jnp.exp(lg[:, None, None] * jnp.maximum(rel, 0.0)), 0.0)
    o_intra = jnp.einsum('nbhij,nbhje->nbhie', jnp.einsum('nbhid,nbhjd->nbhij', qc, kc) * decay, vc)
    k_w = jnp.exp(lg[:, None] * (C - 1 - pos))[:, :, None]
    q_w = jnp.exp(lg[:, None] * (pos + 1))[:, :, None]
    c_dec = jnp.exp(lg * C)[:, None, None]
    kv = jnp.einsum('nbhjd,nbhje->nbhde', kc * k_w, vc)

    def step(S, kv_i):
        return c_dec * S + kv_i, S

    s_final, s_prev = lax.scan(step, s0, kv)
    o_inter = jnp.einsum('nbhid,nbhde->nbhie', qc * q_w, s_prev)
    return from_chunks(o_intra + o_inter), s_final


def retention_log_gamma():
    h = jnp.arange(RET_HEADS, dtype=jnp.float32)
    d = jnp.arange(2, dtype=jnp.float32)[:, None]
    return jnp.log1p(-(2.0 ** (-(RET_DECAY_BASE + h + RET_DIR_OFFSET * d))))


def softmax_attention(q, k, v):
    s = jnp.einsum('bqhd,bkhd->bhqk', q, k).astype(jnp.float32) * MLA_SCALE
    p = jax.nn.softmax(s, axis=-1).astype(v.dtype)
    return jnp.einsum('bhqk,bkhd->bqhd', p, v)


def mla_attention_latent(q, k_all, v_all):
    B, N, H, dq = q.shape
    nb = N // Q_BLOCK
    qb = q.reshape(B, nb, Q_BLOCK, H, dq).swapaxes(0, 1)
    ob = lax.map(lambda qi: softmax_attention(qi, k_all, v_all), qb)
    return ob.swapaxes(0, 1).reshape(B, N, H * v_all.shape[-1])


def gdn_mixer(qkv, z, ab, qkv_c, z_c, ab_c, conv_w, A_log, dt_bias, norm_w, ctx_out):
    f32 = jnp.float32

    def prep(qkv, ab):
        B, N = qkv.shape[:2]
        qkv = jax.nn.silu(dwconv(qkv, conv_w))
        q = l2norm(qkv[..., :GDN_QK].reshape(B, N, GDN_HEADS, GDN_DK)) * GDN_DK ** -0.5
        k = l2norm(qkv[..., GDN_QK:2 * GDN_QK].reshape(B, N, GDN_HEADS, GDN_DK))
        v = qkv[..., 2 * GDN_QK:].reshape(B, N, GDN_HEADS, GDN_DV)
        ab = ab.astype(f32).reshape(B, N, 2, 2, GDN_HEADS)
        g = -jnp.exp(A_log.astype(f32)) * jax.nn.softplus(ab[:, :, 0] + dt_bias.astype(f32))
        beta = jax.nn.sigmoid(ab[:, :, 1])
        return q, k, v, g, beta

    def pick(seqs, d, rev):
        q, k, v, g, beta = seqs
        out = (q, k, v, g[:, :, d], beta[:, :, d])
        return tuple(flip(t) for t in out) if rev else out

    lat = prep(qkv, ab)
    cx = prep(qkv_c, ab_c)
    s0 = jnp.zeros((qkv.shape[0], GDN_HEADS, GDN_DK, GDN_DV), f32)
    oc_f, sc_f = gdn_chunked(*pick(cx, 0, False), s0)
    ol_f, _ = gdn_chunked(*pick(lat, 0, False), sc_f)
    oc_b, sc_b = gdn_chunked(*pick(cx, 1, True), s0)
    ol_b, _ = gdn_chunked(*pick(lat, 1, True), sc_b)
    out = gated_head_norm(ol_f + flip(ol_b), z, norm_w)
    out_c = gated_head_norm(oc_f + flip(oc_b), z_c, norm_w) if ctx_out else None
    return out, out_c


def mla_mixer(cq, ckv, kr, cq_c, ckv_c, kr_c, q_norm, w_uq, kv_norm, w_ukv, rope, ctx_out):
    def queries(cq, rope):
        B, N = cq.shape[:2]
        q = (rms_norm(cq, q_norm) @ w_uq).reshape(B, N, MLA_HEADS, MLA_NOPE + MLA_ROPE)
        q_nope, q_rope = q[..., :MLA_NOPE], q[..., MLA_NOPE:]
        if rope is not None:
            q_rope = apply_rope(q_rope, *rope)
        return jnp.concatenate([q_nope, q_rope], axis=-1)

    def keys_values(ckv, kr, rope):
        B, N = ckv.shape[:2]
        kv = (rms_norm(ckv, kv_norm) @ w_ukv).reshape(B, N, MLA_HEADS, MLA_NOPE + MLA_DV)
        k_rope = kr[:, :, None, :]
        if rope is not None:
            k_rope = apply_rope(k_rope, *rope)
        k = jnp.concatenate([kv[..., :MLA_NOPE], jnp.broadcast_to(k_rope, (B, N, MLA_HEADS, MLA_ROPE))], axis=-1)
        return k, kv[..., MLA_NOPE:]

    k_l, v_l = keys_values(ckv, kr, rope)
    k_c, v_c = keys_values(ckv_c, kr_c, None)
    out = mla_attention_latent(queries(cq, rope), jnp.concatenate([k_l, k_c], axis=1),
                               jnp.concatenate([v_l, v_c], axis=1))
    out_c = None
    if ctx_out:
        B, M = cq_c.shape[:2]
        out_c = softmax_attention(queries(cq_c, None), k_c, v_c).reshape(B, M, MLA_OUT)
    return out, out_c


def ret_mixer(qkvg, qkvg_c, norm_w, rope, ctx_out):
    def prep(t, rope):
        B, N = t.shape[:2]
        q = t[..., :RET_QK].reshape(B, N, RET_HEADS, RET_DK)
        k = t[..., RET_QK:2 * RET_QK].reshape(B, N, RET_HEADS, RET_DK)
        v = t[..., 2 * RET_QK:2 * RET_QK + RET_V].reshape(B, N, RET_HEADS, RET_DV)
        g = t[..., 2 * RET_QK + RET_V:]
        if rope is not None:
            q = apply_rope(q, *rope)
            k = apply_rope(k, *rope)
        return q * RET_DK ** -0.5, k, v, g

    q_l, k_l, v_l, g_l = prep(qkvg, rope)
    q_c, k_c, v_c, g_c = prep(qkvg_c, None)
    lg = retention_log_gamma()
    s0 = jnp.zeros((qkvg.shape[0], RET_HEADS, RET_DK, RET_DV), jnp.float32)
    oc_f, sc_f = retention_chunked(q_c, k_c, v_c, lg[0], s0)
    ol_f, _ = retention_chunked(q_l, k_l, v_l, lg[0], sc_f)
    oc_b, sc_b = retention_chunked(flip(q_c), flip(k_c), flip(v_c), lg[1], s0)
    ol_b, _ = retention_chunked(flip(q_l), flip(k_l), flip(v_l), lg[1], sc_b)
    w = norm_w.reshape(RET_HEADS, RET_DV)
    out = gated_head_norm(ol_f + flip(ol_b), g_l, w)
    out_c = gated_head_norm(oc_f + flip(oc_b), g_c, w) if ctx_out else None
    return out, out_c


def token_mixer(h, hc, w_in, gdn_conv_w, gdn_A_log, gdn_dt_bias, gdn_norm_w, mla_q_norm, mla_w_uq,
                mla_kv_norm, mla_w_ukv, ret_norm_w, w_br_gdn, w_br_mla, w_br_ret, w_out,
                rope_mla, rope_ret, ctx_out):
    cuts = np.cumsum(IN_SPLITS)[:-1].tolist()
    p = jnp.split(h @ w_in, cuts, axis=-1)
    pc = jnp.split(hc @ w_in, cuts, axis=-1)
    o_a, oc_a = gdn_mixer(p[0], p[1], p[2], pc[0], pc[1], pc[2], gdn_conv_w, gdn_A_log, gdn_dt_bias,
                          gdn_norm_w, ctx_out)
    o_b, oc_b = mla_mixer(p[3], p[4], p[5], pc[3], pc[4], pc[5], mla_q_norm, mla_w_uq, mla_kv_norm,
                          mla_w_ukv, rope_mla, ctx_out)
    o_c, oc_c = ret_mixer(p[6], pc[6], ret_norm_w, rope_ret, ctx_out)

    def merge(oa, ob, oc, gate_logits):
        gate = jax.nn.sigmoid(gate_logits.astype(jnp.float32)).astype(oa.dtype)
        ga, gb, gc = jnp.split(gate, N_BRANCH, axis=-1)
        return (ga * (oa @ w_br_gdn) + gb * (ob @ w_br_mla) + gc * (oc @ w_br_ret)) @ w_out

    y = merge(o_a, o_b, o_c, p[7])
    yc = merge(oc_a, oc_b, oc_c, pc[7]) if ctx_out else None
    return y, yc


def conv_ffn(h, w_up, conv_w, conv_b, w_down):
    u = dwconv(h @ w_up, conv_w) + conv_b
    gate, val = jnp.split(u, 2, axis=-1)
    return (jax.nn.silu(gate) * val) @ w_down


def setup_inputs(seed: int = 0) -> dict:
    key = jax.random.key(seed)
    ks = jax.random.split(key, 27)
    it = iter(range(27))
    f32 = jnp.float32
    L = DEPTH

    def nrm(shape, std):
        return std * jax.random.normal(ks[next(it)], shape, f32)

    def gain(shape):
        return 1.0 + 0.02 * jax.random.normal(ks[next(it)], shape, f32)

    x = nrm((BATCH, SEQ, D_MODEL), 1.0)
    c = nrm((BATCH, D_MODEL), 1.0)
    ctx = nrm((BATCH, CTX_LEN, D_MODEL), 1.0)
    c_ctx = nrm((D_MODEL,), 1.0)
    ada_w = nrm((L, D_MODEL, 6 * D_MODEL), 0.5 * D_MODEL ** -0.5)
    ada_b = nrm((L, 6 * D_MODEL), 0.02)
    norm1_w = gain((L, D_MODEL))
    w_in = nrm((L, D_MODEL, IN_COLS), D_MODEL ** -0.5)
    gdn_conv_w = nrm((L, SHORT_CONV, 2 * GDN_QK + GDN_V), SHORT_CONV ** -0.5)
    gdn_A_log = jnp.log(jax.random.uniform(ks[next(it)], (L, 2, GDN_HEADS), f32, 1.0, 16.0))
    dt = jnp.exp(jax.random.uniform(ks[next(it)], (L, 2, GDN_HEADS), f32, math.log(1e-3), math.log(1e-1)))
    gdn_dt_bias = dt + jnp.log(-jnp.expm1(-dt))
    gdn_norm_w = gain((L, GDN_DV))
    mla_q_norm = gain((L, MLA_Q_RANK))
    mla_w_uq = nrm((L, MLA_Q_RANK, MLA_HEADS * (MLA_NOPE + MLA_ROPE)), MLA_Q_RANK ** -0.5)
    mla_kv_norm = gain((L, MLA_KV_RANK))
    mla_w_ukv = nrm((L, MLA_KV_RANK, MLA_HEADS * (MLA_NOPE + MLA_DV)), MLA_KV_RANK ** -0.5)
    ret_norm_w = gain((L, RET_V))
    w_br_gdn = nrm((L, GDN_V, D_MODEL), GDN_V ** -0.5)
    w_br_mla = nrm((L, MLA_OUT, D_MODEL), MLA_OUT ** -0.5)
    w_br_ret = nrm((L, RET_V, D_MODEL), RET_V ** -0.5)
    w_out = nrm((L, D_MODEL, D_MODEL), D_MODEL ** -0.5)
    norm2_w = gain((L, D_MODEL))
    ffn_w_up = nrm((L, D_MODEL, 2 * D_FF), D_MODEL ** -0.5)
    ffn_conv_w = nrm((L, FFN_CONV, 2 * D_FF), FFN_CONV ** -0.5)
    ffn_conv_b = nrm((L, 2 * D_FF), 0.02)
    ffn_w_down = nrm((L, D_FF, D_MODEL), D_FF ** -0.5)
    final_norm_w = gain((D_MODEL,))
    return {'x': x, 'c': c, 'ctx': ctx, 'c_ctx': c_ctx, 'ada_w': ada_w, 'ada_b': ada_b,
            'norm1_w': norm1_w, 'w_in': w_in, 'gdn_conv_w': gdn_conv_w, 'gdn_A_log': gdn_A_log,
            'gdn_dt_bias': gdn_dt_bias, 'gdn_norm_w': gdn_norm_w, 'mla_q_norm': mla_q_norm,
            'mla_w_uq': mla_w_uq, 'mla_kv_norm': mla_kv_norm, 'mla_w_ukv': mla_w_ukv,
            'ret_norm_w': ret_norm_w, 'w_br_gdn': w_br_gdn, 'w_br_mla': w_br_mla, 'w_br_ret': w_br_ret,
            'w_out': w_out, 'norm2_w': norm2_w, 'ffn_w_up': ffn_w_up, 'ffn_conv_w': ffn_conv_w,
            'ffn_conv_b': ffn_conv_b, 'ffn_w_down': ffn_w_down, 'final_norm_w': final_norm_w}


def reference(x, c, ctx, c_ctx, ada_w, ada_b, norm1_w, w_in, gdn_conv_w, gdn_A_log, gdn_dt_bias,
              gdn_norm_w, mla_q_norm, mla_w_uq, mla_kv_norm, mla_w_ukv, ret_norm_w, w_br_gdn,
              w_br_mla, w_br_ret, w_out, norm2_w, ffn_w_up, ffn_conv_w, ffn_conv_b, ffn_w_down,
              final_norm_w):
    n = x.shape[1]
    rope_mla = axial_rope(n, MLA_ROPE)
    rope_ret = axial_rope(n, RET_DK)
    c_act = jax.nn.silu(c)
    cc_act = jax.nn.silu(c_ctx)
    xc = ctx
    for l in range(DEPTH):
        ctx_out = l < DEPTH - 1
        mod = (c_act @ ada_w[l] + ada_b[l])[:, None, :]
        modc = cc_act @ ada_w[l] + ada_b[l]
        sh1, sc1, g1, sh2, sc2, g2 = jnp.split(mod, 6, axis=-1)
        sh1c, sc1c, g1c, sh2c, sc2c, g2c = jnp.split(modc, 6, axis=-1)
        h = modulate(rms_norm(x, norm1_w[l]), sh1, sc1)
        hc = modulate(rms_norm(xc, norm1_w[l]), sh1c, sc1c)
        y, yc = token_mixer(h, hc, w_in[l], gdn_conv_w[l], gdn_A_log[l], gdn_dt_bias[l], gdn_norm_w[l],
                            mla_q_norm[l], mla_w_uq[l], mla_kv_norm[l], mla_w_ukv[l], ret_norm_w[l],
                            w_br_gdn[l], w_br_mla[l], w_br_ret[l], w_out[l], rope_mla, rope_ret, ctx_out)
        x = x + g1 * y
        x = x + g2 * conv_ffn(modulate(rms_norm(x, norm2_w[l]), sh2, sc2), ffn_w_up[l], ffn_conv_w[l],
                              ffn_conv_b[l], ffn_w_down[l])
        if ctx_out:
            xc = xc + g1c * yc
            xc = xc + g2c * conv_ffn(modulate(rms_norm(xc, norm2_w[l]), sh2c, sc2c), ffn_w_up[l],
                                     ffn_conv_w[l], ffn_conv_b[l], ffn_w_down[l])
    return rms_norm(x, final_norm_w)
```

```python
import functools
import math

import jax
import jax.numpy as jnp
from jax import lax
from jax.experimental import pallas as pl
from jax.experimental.pallas import tpu as pltpu

F32 = jnp.float32
BF16 = jnp.bfloat16
HIGHEST = lax.Precision.HIGHEST

EPS = 1e-6
GRID_W = 64
ROPE_BASE = 10000.0
HEADS = 4
HEAD_DIM = 128
CHUNK = 64
MLA_Q_RANK = 384
MLA_KV_RANK = 256
MLA_NOPE = 128
MLA_ROPE = 64
MLA_SCALE = (MLA_NOPE + MLA_ROPE) ** -0.5
MLA_HEAD_PAD = 256
RET_DECAY_BASE = 5.0
RET_DIR_OFFSET = 0.5
QK_SCALE = HEAD_DIM ** -0.5
NEG_BIG = -1e30

P_MLA = 0
P_KR = 640
P_KR_ROT = 768
P_AB = 896
P_GATES = 1024
P_RET = 4096
P_Z = 6144
P_COLS = 6656

V7X_VMEM_LIMIT = 56 * 1024 * 1024
SEQ_BLOCK = 256
HALO = 16


def _cparams(sem):
    return pltpu.CompilerParams(dimension_semantics=sem, vmem_limit_bytes=V7X_VMEM_LIMIT)


def _bdot(a, b):
    return jnp.dot(a.astype(BF16), b.astype(BF16), preferred_element_type=F32)


def _bdot_nt(a, b):
    return lax.dot_general(a.astype(BF16), b.astype(BF16), (((1,), (1,)), ((), ())),
                           preferred_element_type=F32)


def _bdot_tn(a, b):
    return lax.dot_general(a.astype(BF16), b.astype(BF16), (((0,), (0,)), ((), ())),
                           preferred_element_type=F32)


def _rms(x, w):
    return x * lax.rsqrt(jnp.mean(x * x, axis=-1, keepdims=True) + EPS) * w


def _silu(x):
    return x * (1.0 / (1.0 + jnp.exp(-x)))


def _sigmoid(x):
    return 1.0 / (1.0 + jnp.exp(-x))


def _mod_kernel(c_ref, w_ref, b_ref, o_ref):
    a = _silu(c_ref[...])
    o_ref[0] = _bdot(a, w_ref[0]) + b_ref[0]


def _modulation(cc, ada_w, ada_b):
    L, D, D6 = ada_w.shape
    R = cc.shape[0]
    tn = 1536
    return pl.pallas_call(
        _mod_kernel,
        grid=(L, D6 // tn),
        in_specs=[pl.BlockSpec((R, D), lambda l, j: (0, 0)),
                  pl.BlockSpec((1, D, tn), lambda l, j: (l, 0, j)),
                  pl.BlockSpec((1, 1, tn), lambda l, j: (l, 0, j))],
        out_specs=pl.BlockSpec((1, R, tn), lambda l, j: (l, 0, j)),
        out_shape=jax.ShapeDtypeStruct((L, R, D6), F32),
        compiler_params=_cparams(("parallel", "parallel")),
        name="modulation",
    )(cc, ada_w, ada_b.reshape(L, 1, D6))


def _norm_mod_kernel(x_ref, mod_ref, nw_ref, h_ref):
    m = mod_ref[0]
    h = _rms(x_ref[0], nw_ref[...]) * (1.0 + m[1:2]) + m[0:1]
    h_ref[0] = h.astype(BF16)


def _norm_mod(x9, mod9, nw):
    S, N, D = x9.shape
    tm = min(512, N)
    return pl.pallas_call(
        _norm_mod_kernel,
        grid=(S, N // tm),
        in_specs=[pl.BlockSpec((1, tm, D), lambda b, i: (b, i, 0)),
                  pl.BlockSpec((1, 6, D), lambda b, i: (b, 0, 0)),
                  pl.BlockSpec((1, D), lambda b, i: (0, 0))],
        out_specs=pl.BlockSpec((1, tm, D), lambda b, i: (b, i, 0)),
        out_shape=jax.ShapeDtypeStruct((S, N, D), BF16),
        compiler_params=_cparams(("parallel", "parallel")),
        name="norm_mod",
    )(x9, mod9, nw.reshape(1, D))


def _conv3_rows(u, uh, cw, row0, seq_len):
    tm = u.shape[0]
    loc = lax.broadcasted_iota(jnp.int32, (tm, 1), 0)
    pos = (loc + row0) & (seq_len - 1)
    dn = pltpu.roll(u, 1, 0)
    dn = jnp.where(loc == 0, uh[HALO - 1:HALO], dn)
    dn = jnp.where(pos == 0, 0.0, dn)
    up = pltpu.roll(u, tm - 1, 0)
    up = jnp.where(loc == tm - 1, uh[HALO:HALO + 1], up)
    up = jnp.where(pos == seq_len - 1, 0.0, up)
    return cw[0:1] * dn + cw[1:2] * u + cw[2:3] * up


def _halo_specs(tm, N, D):
    nb = tm // HALO
    last = N // HALO - 1
    return [pl.BlockSpec((1, tm, D), lambda b, i, j: (b, i, 0)),
            pl.BlockSpec((1, HALO, D), lambda b, i, j: (b, jnp.maximum(i * nb - 1, 0), 0)),
            pl.BlockSpec((1, HALO, D), lambda b, i, j: (b, jnp.minimum((i + 1) * nb, last), 0))]


def _gdn_proj_kernel(h_ref, hp_ref, hn_ref, w_ref, cw_ref, o_ref, *, nlat, seq_lat, seq_ctx):
    b, i, j = pl.program_id(0), pl.program_id(1), pl.program_id(2)
    tm = h_ref.shape[1]
    w = w_ref[...]
    u = jnp.dot(h_ref[0], w, preferred_element_type=F32)
    hh = jnp.concatenate([hp_ref[0], hn_ref[0]], axis=0)
    uh = jnp.dot(hh, w, preferred_element_type=F32)
    seq_len = jnp.where(b == nlat, seq_ctx, seq_lat)
    y = _silu(_conv3_rows(u, uh, cw_ref[...], i * tm, seq_len))
    qk_scale = jnp.where(j == 0, QK_SCALE, 1.0)
    for h in range(HEADS):
        yh = y[:, h * HEAD_DIM:(h + 1) * HEAD_DIM]
        r = lax.rsqrt(jnp.sum(yh * yh, axis=-1, keepdims=True) + EPS)
        f = jnp.where(j == 2, 1.0, r * qk_scale)
        o_ref[0, :, h * HEAD_DIM:(h + 1) * HEAD_DIM] = yh * f


def _gdn_proj(h9, w_qkv, conv_w, nlat, seq_ctx):
    S, N, D = h9.shape
    C = w_qkv.shape[1]
    tm = min(1024, N)
    tn = HEADS * HEAD_DIM
    return pl.pallas_call(
        functools.partial(_gdn_proj_kernel, nlat=nlat, seq_lat=N, seq_ctx=seq_ctx),
        grid=(S, N // tm, C // tn),
        in_specs=_halo_specs(tm, N, D) + [
            pl.BlockSpec((D, tn), lambda b, i, j: (0, j)),
            pl.BlockSpec((3, tn), lambda b, i, j: (0, j))],
        out_specs=pl.BlockSpec((1, tm, tn), lambda b, i, j: (b, i, j)),
        out_shape=jax.ShapeDtypeStruct((S, N, C), F32),
        compiler_params=_cparams(("parallel", "parallel", "arbitrary")),
        name="gdn_proj",
    )(h9, h9, h9, w_qkv, conv_w)


def _matmul_kernel(h_ref, w_ref, o_ref):
    o_ref[0] = jnp.dot(h_ref[0], w_ref[...], preferred_element_type=F32)


def _proj_rest(h9, w_p):
    S, N, D = h9.shape
    C = w_p.shape[1]
    tm = min(1024, N)
    tn = 512
    return pl.pallas_call(
        _matmul_kernel,
        grid=(S, N // tm, C // tn),
        in_specs=[pl.BlockSpec((1, tm, D), lambda b, i, j: (b, i, 0)),
                  pl.BlockSpec((D, tn), lambda b, i, j: (0, j))],
        out_specs=pl.BlockSpec((1, tm, tn), lambda b, i, j: (b, i, j)),
        out_shape=jax.ShapeDtypeStruct((S, N, C), F32),
        compiler_params=_cparams(("parallel", "parallel", "arbitrary")),
        name="proj_rest",
    )(h9, w_p)


def _seq_maps(nlat, n_ctx_blk, n_lat_blk):
    def fwd(b, s):
        is_ctx = s < n_ctx_blk
        return jnp.where(is_ctx, nlat, b), jnp.where(is_ctx, b * n_ctx_blk + s, s - n_ctx_blk)

    def bwd(b, s):
        is_ctx = s < n_ctx_blk
        return (jnp.where(is_ctx, nlat, b),
                jnp.where(is_ctx, b * n_ctx_blk + n_ctx_blk - 1 - s, n_lat_blk - 1 - (s - n_ctx_blk)))

    return fwd, bwd


def _tri_masks(n, backward):
    r = lax.broadcasted_iota(jnp.int32, (n, n), 0)
    c = lax.broadcasted_iota(jnp.int32, (n, n), 1)
    if backward:
        return r <= c, r < c
    return r >= c, r > c


def _gdn_direction(d, q_ref, k_ref, v_ref, ab_ref, abt_ref, alog_c, dtb_c, alog_r, dtb_r, s_ref, o_ref):
    T = q_ref.shape[1]
    nchunk = T // CHUNK
    backward = d == 1
    ab = ab_ref[0]
    abt = abt_ref[0]

    def softplus(x):
        return jnp.maximum(x, 0.0) + jnp.log1p(jnp.exp(-jnp.abs(x)))

    g_c = -jnp.exp(alog_c) * softplus(ab + dtb_c)
    g_r = -jnp.exp(alog_r) * softplus(abt + dtb_r)
    beta_c = _sigmoid(ab)
    ri = lax.broadcasted_iota(jnp.int32, (T, T), 0)
    ci = lax.broadcasted_iota(jnp.int32, (T, T), 1)
    same = (ri // CHUNK) == (ci // CHUNK)
    cs = jnp.where(same & ((ci >= ri) if backward else (ci <= ri)), 1.0, 0.0)
    gcum_c = jnp.dot(cs, g_c, precision=HIGHEST, preferred_element_type=F32)
    gcum_r = lax.dot_general(g_r, cs, (((1,), (1,)), ((), ())), precision=HIGHEST,
                             preferred_element_type=F32)
    eg_c = jnp.exp(gcum_c)
    incl, strict = _tri_masks(CHUNK, backward)
    eye = jnp.where(lax.broadcasted_iota(jnp.int32, (CHUNK, CHUNK), 0)
                    == lax.broadcasted_iota(jnp.int32, (CHUNK, CHUNK), 1), 1.0, 0.0)

    for h in range(HEADS):
        ia, ib = 4 * d + h, 8 + 4 * d + h
        hs = slice(h * HEAD_DIM, (h + 1) * HEAD_DIM)
        S = s_ref[h]
        order = range(nchunk - 1, -1, -1) if backward else range(nchunk)
        for c in order:
            rs = slice(c * CHUNK, (c + 1) * CHUNK)
            q = q_ref[0, rs, hs]
            k = k_ref[0, rs, hs]
            v = v_ref[0, rs, hs]
            gc = gcum_c[rs, ia:ia + 1]
            gr = gcum_r[ia:ia + 1, rs]
            egc = eg_c[rs, ia:ia + 1]
            beta = beta_c[rs, ib:ib + 1]
            last = c * CHUNK if backward else (c + 1) * CHUNK - 1
            gtot = gcum_c[last:last + 1, ia:ia + 1]
            diff = gc - gr
            decay = jnp.exp(jnp.where(incl, diff, NEG_BIG))
            decay_s = jnp.where(strict, decay, 0.0)
            kb = k * beta
            qk = _bdot_nt(jnp.concatenate([q, kb], axis=0), k)
            attn = qk[:CHUNK] * decay
            x = -(qk[CHUNK:] * decay_s)
            t = eye + x
            p = x
            for _ in range(int(math.log2(CHUNK)) - 1):
                p = _bdot(p, p)
                t = t + _bdot(t, p)
            sol = _bdot(t, jnp.concatenate([v * beta, kb * egc], axis=1))
            u, w = sol[:, :HEAD_DIM], sol[:, HEAD_DIM:]
            ws = _bdot(jnp.concatenate([w, q * egc], axis=0), S)
            v_new = u - ws[:CHUNK]
            o_ref[0, rs, hs] = ws[CHUNK:] + _bdot(attn, v_new)
            k_dec = k * jnp.exp(gtot - gc)
            S = jnp.exp(gtot) * S + _bdot_tn(k_dec, v_new)
        s_ref[h] = S


def _gdn_kernel(qf, kf, vf, abf, abtf, qb, kb, vb, abb, abtb, alog_c, dtb_c, alog_r, dtb_r,
                of_ref, ob_ref, sf_ref, sb_ref):
    @pl.when(pl.program_id(1) == 0)
    def _():
        sf_ref[...] = jnp.zeros_like(sf_ref)
        sb_ref[...] = jnp.zeros_like(sb_ref)

    ac, dc, ar, dr = alog_c[...], dtb_c[...], alog_r[...], dtb_r[...]
    _gdn_direction(0, qf, kf, vf, abf, abtf, ac, dc, ar, dr, sf_ref, of_ref)
    _gdn_direction(1, qb, kb, vb, abb, abtb, ac, dc, ar, dr, sb_ref, ob_ref)


def _gdn_scan(qkv, P, abT, A_log, dt_bias, nlat, seq_ctx):
    S9, N, _ = qkv.shape
    T = SEQ_BLOCK
    ncb, nlb = seq_ctx // T, N // T
    fwd, bwd = _seq_maps(nlat, ncb, nlb)
    W = HEADS * HEAD_DIM

    def specs(m):
        return [pl.BlockSpec((1, T, W), lambda b, s, m=m: (*m(b, s), 0)),
                pl.BlockSpec((1, T, W), lambda b, s, m=m: (*m(b, s), 1)),
                pl.BlockSpec((1, T, W), lambda b, s, m=m: (*m(b, s), 2)),
                pl.BlockSpec((1, T, 128), lambda b, s, m=m: (*m(b, s), P_AB // 128)),
                pl.BlockSpec((1, 16, T), lambda b, s, m=m: (m(b, s)[0], 0, m(b, s)[1]))]

    small = lambda shape: pl.BlockSpec(shape, lambda b, s: (0, 0))
    alog = A_log.reshape(-1)
    dtb = dt_bias.reshape(-1)
    pad_c = lambda t: jnp.zeros((1, 128), F32).at[0, :8].set(t)
    pad_r = lambda t: jnp.zeros((16, 1), F32).at[:8, 0].set(t)
    out_sds = jax.ShapeDtypeStruct((S9, N, W), F32)
    return pl.pallas_call(
        _gdn_kernel,
        grid=(nlat, ncb + nlb),
        in_specs=specs(fwd) + specs(bwd) + [small((1, 128)), small((1, 128)), small((16, 1)), small((16, 1))],
        out_specs=[pl.BlockSpec((1, T, W), lambda b, s: (*fwd(b, s), 0)),
                   pl.BlockSpec((1, T, W), lambda b, s: (*bwd(b, s), 0))],
        out_shape=[out_sds, out_sds],
        scratch_shapes=[pltpu.VMEM((HEADS, HEAD_DIM, HEAD_DIM), F32),
                        pltpu.VMEM((HEADS, HEAD_DIM, HEAD_DIM), F32)],
        compiler_params=_cparams(("parallel", "arbitrary")),
        name="gdn_scan",
    )(qkv, qkv, qkv, P, abT, qkv, qkv, qkv, P, abT, pad_c(alog), pad_c(dtb), pad_r(alog), pad_r(dtb))


def _ret_log_gamma(d, h):
    return math.log1p(-(2.0 ** (-(RET_DECAY_BASE + h + RET_DIR_OFFSET * d))))


def _ret_direction(d, q_ref, k_ref, v_ref, cos_ref, sin_ref, s_ref, o_ref):
    T = q_ref.shape[1]
    nchunk = T // CHUNK
    backward = d == 1
    cos = cos_ref[0]
    sin = sin_ref[0]
    incl, _ = _tri_masks(CHUNK, backward)
    r = lax.broadcasted_iota(jnp.int32, (CHUNK, CHUNK), 0)
    c = lax.broadcasted_iota(jnp.int32, (CHUNK, CHUNK), 1)
    dist = jnp.abs(r - c).astype(F32)
    pos = lax.broadcasted_iota(jnp.int32, (CHUNK, 1), 0).astype(F32)
    trav = (CHUNK - 1 - pos) if backward else pos

    def rope(t):
        return t * cos + pltpu.roll(t, HEAD_DIM // 2, 1) * sin

    for h in range(HEADS):
        lg = _ret_log_gamma(d, h)
        decay = jnp.where(incl, jnp.exp(lg * dist), 0.0)
        k_w = jnp.exp(lg * (CHUNK - 1 - trav))
        q_w = jnp.exp(lg * (trav + 1.0))
        c_dec = math.exp(lg * CHUNK)
        hs = slice(h * HEAD_DIM, (h + 1) * HEAD_DIM)
        S = s_ref[h]
        order = range(nchunk - 1, -1, -1) if backward else range(nchunk)
        for ch in order:
            rs = slice(ch * CHUNK, (ch + 1) * CHUNK)
            cs_, sn_ = cos[rs], sin[rs]
            q = q_ref[0, rs, hs]
            k = k_ref[0, rs, hs]
            q = (q * cs_ + pltpu.roll(q, HEAD_DIM // 2, 1) * sn_) * QK_SCALE
            k = k * cs_ + pltpu.roll(k, HEAD_DIM // 2, 1) * sn_
            v = v_ref[0, rs, hs]
            a = _bdot_nt(q, k) * decay
            o_ref[0, rs, hs] = _bdot(a, v) + _bdot(q * q_w, S)
            S = c_dec * S + _bdot_tn(k * k_w, v)
        s_ref[h] = S
    del rope


def _ret_kernel(qf, kf, vf, cf, sf, qb, kb, vb, cb, sb, of_ref, ob_ref, stf_ref, stb_ref):
    @pl.when(pl.program_id(1) == 0)
    def _():
        stf_ref[...] = jnp.zeros_like(stf_ref)
        stb_ref[...] = jnp.zeros_like(stb_ref)

    _ret_direction(0, qf, kf, vf, cf, sf, stf_ref, of_ref)
    _ret_direction(1, qb, kb, vb, cb, sb, stb_ref, ob_ref)


def _ret_scan(P, cos2, sin2, nlat, seq_ctx):
    S9, N, _ = P.shape
    T = SEQ_BLOCK
    ncb, nlb = seq_ctx // T, N // T
    fwd, bwd = _seq_maps(nlat, ncb, nlb)
    W = HEADS * HEAD_DIM
    c0 = P_RET // W

    def tab(m):
        def f(b, s):
            slab, blk = m(b, s)
            is_ctx = slab == nlat
            return jnp.where(is_ctx, 1, 0), jnp.where(is_ctx, 0, blk), 0
        return f

    def specs(m):
        return [pl.BlockSpec((1, T, W), lambda b, s, m=m: (*m(b, s), c0)),
                pl.BlockSpec((1, T, W), lambda b, s, m=m: (*m(b, s), c0 + 1)),
                pl.BlockSpec((1, T, W), lambda b, s, m=m: (*m(b, s), c0 + 2)),
                pl.BlockSpec((1, T, HEAD_DIM), tab(m)),
                pl.BlockSpec((1, T, HEAD_DIM), tab(m))]

    out_sds = jax.ShapeDtypeStruct((S9, N, W), F32)
    return pl.pallas_call(
        _ret_kernel,
        grid=(nlat, ncb + nlb),
        in_specs=specs(fwd) + specs(bwd),
        out_specs=[pl.BlockSpec((1, T, W), lambda b, s: (*fwd(b, s), 0)),
                   pl.BlockSpec((1, T, W), lambda b, s: (*bwd(b, s), 0))],
        out_shape=[out_sds, out_sds],
        scratch_shapes=[pltpu.VMEM((HEADS, HEAD_DIM, HEAD_DIM), F32),
                        pltpu.VMEM((HEADS, HEAD_DIM, HEAD_DIM), F32)],
        compiler_params=_cparams(("parallel", "arbitrary")),
        name="ret_scan",
    )(P, P, P, cos2, sin2, P, P, P, cos2, sin2)


def _mla_prep_kernel(p_ref, qn_ref, wq_ref, wqr_ref, kvn_ref, wk_ref, wv_ref, cq_ref, sq_ref, ck_ref, sk_ref,
                     q_ref, k_ref, v_ref):
    p = p_ref[0]
    cq = _rms(p[:, :MLA_Q_RANK], qn_ref[...]).astype(BF16)
    ckv = _rms(p[:, MLA_Q_RANK:MLA_Q_RANK + MLA_KV_RANK], kvn_ref[...]).astype(BF16)
    cosq, sinq = cq_ref[0], sq_ref[0]
    q = jnp.dot(cq, wq_ref[...], preferred_element_type=F32)
    qr = jnp.dot(cq, wqr_ref[...], preferred_element_type=F32)
    for h in range(HEADS):
        hs = slice(h * MLA_HEAD_PAD, (h + 1) * MLA_HEAD_PAD)
        q_ref[0, :, hs] = (q[:, hs] * cosq + qr[:, hs] * sinq).astype(BF16)
    kn = jnp.dot(ckv, wk_ref[...], preferred_element_type=F32)
    v_ref[0] = jnp.dot(ckv, wv_ref[...], preferred_element_type=F32).astype(BF16)
    krope = (p[:, P_KR:P_KR + MLA_ROPE] * ck_ref[0] + p[:, P_KR_ROT:P_KR_ROT + MLA_ROPE] * sk_ref[0]).astype(BF16)
    zpad = jnp.zeros((p.shape[0], MLA_HEAD_PAD - MLA_NOPE - MLA_ROPE), BF16)
    for h in range(HEADS):
        o = h * MLA_HEAD_PAD
        k_ref[0, :, o:o + MLA_NOPE] = kn[:, h * MLA_NOPE:(h + 1) * MLA_NOPE].astype(BF16)
        k_ref[0, :, o + MLA_NOPE:o + MLA_NOPE + MLA_ROPE] = krope
        k_ref[0, :, o + MLA_NOPE + MLA_ROPE:o + MLA_HEAD_PAD] = zpad


def _mla_prep(P, q_norm, wq, wq_rot, kv_norm, wk, wv, cosq, sinq, cosk, sink, nlat):
    S9, N, _ = P.shape
    tm = min(512, N)
    QW = HEADS * MLA_HEAD_PAD
    VW = HEADS * HEAD_DIM
    full = lambda a: pl.BlockSpec(a.shape, lambda b, i: (0,) * a.ndim)
    tab = lambda w: pl.BlockSpec((1, tm, w), lambda b, i: (jnp.where(b == nlat, 1, 0), jnp.where(b == nlat, 0, i), 0))
    qn = q_norm.reshape(1, -1)
    kvn = kv_norm.reshape(1, -1)
    return pl.pallas_call(
        _mla_prep_kernel,
        grid=(S9, N // tm),
        in_specs=[pl.BlockSpec((1, tm, 1024), lambda b, i: (b, i, P_MLA // 1024)),
                  full(qn), full(wq), full(wq_rot), full(kvn), full(wk), full(wv),
                  tab(MLA_HEAD_PAD), tab(MLA_HEAD_PAD), tab(MLA_ROPE), tab(MLA_ROPE)],
        out_specs=[pl.BlockSpec((1, tm, QW), lambda b, i: (b, i, 0)),
                   pl.BlockSpec((1, tm, QW), lambda b, i: (b, i, 0)),
                   pl.BlockSpec((1, tm, VW), lambda b, i: (b, i, 0))],
        out_shape=[jax.ShapeDtypeStruct((S9, N, QW), BF16),
                   jax.ShapeDtypeStruct((S9, N, QW), BF16),
                   jax.ShapeDtypeStruct((S9, N, VW), BF16)],
        compiler_params=_cparams(("parallel", "parallel")),
        name="mla_prep",
    )(P, qn, wq, wq_rot, kvn, wk, wv, cosq, sinq, cosk, sink)


def _mla_attn_kernel(q_ref, kl_ref, kc_ref, vl_ref, vc_ref, o_ref):
    q = q_ref[0]
    s1 = lax.dot_general(q, kl_ref[0], (((1,), (1,)), ((), ())), preferred_element_type=F32)
    s2 = lax.dot_general(q, kc_ref[0], (((1,), (1,)), ((), ())), preferred_element_type=F32)
    m = jnp.maximum(jnp.max(s1, axis=-1, keepdims=True), jnp.max(s2, axis=-1, keepdims=True))
    p1 = jnp.exp((s1 - m) * MLA_SCALE)
    p2 = jnp.exp((s2 - m) * MLA_SCALE)
    l = jnp.sum(p1, axis=-1, keepdims=True) + jnp.sum(p2, axis=-1, keepdims=True)
    o = (jnp.dot(p1.astype(BF16), vl_ref[0], preferred_element_type=F32)
         + jnp.dot(p2.astype(BF16), vc_ref[0], preferred_element_type=F32))
    o_ref[0] = (o / l).astype(BF16)


def _mla_attn(Q, K, V, nlat, seq_ctx):
    S9, N, _ = Q.shape
    tq = min(512, N)
    M = seq_ctx
    return pl.pallas_call(
        _mla_attn_kernel,
        grid=(nlat, HEADS, N // tq),
        in_specs=[pl.BlockSpec((1, tq, MLA_HEAD_PAD), lambda b, h, i: (b, i, h)),
                  pl.BlockSpec((1, N, MLA_HEAD_PAD), lambda b, h, i: (b, 0, h)),
                  pl.BlockSpec((1, M, MLA_HEAD_PAD), lambda b, h, i: (nlat, b, h)),
                  pl.BlockSpec((1, N, HEAD_DIM), lambda b, h, i: (b, 0, h)),
                  pl.BlockSpec((1, M, HEAD_DIM), lambda b, h, i: (nlat, b, h))],
        out_specs=pl.BlockSpec((1, tq, HEAD_DIM), lambda b, h, i: (b, i, h)),
        out_shape=jax.ShapeDtypeStruct((S9, N, HEADS * HEAD_DIM), BF16),
        compiler_params=_cparams(("parallel", "parallel", "arbitrary")),
        name="mla_attn",
    )(Q, K, K, V, V)


def _mla_attn_ctx_kernel(q_ref, kc_ref, vc_ref, o_in_ref, o_ref):
    del o_in_ref
    s = lax.dot_general(q_ref[0], kc_ref[0], (((1,), (1,)), ((), ())), preferred_element_type=F32)
    m = jnp.max(s, axis=-1, keepdims=True)
    p = jnp.exp((s - m) * MLA_SCALE)
    l = jnp.sum(p, axis=-1, keepdims=True)
    o = jnp.dot(p.astype(BF16), vc_ref[0], preferred_element_type=F32)
    o_ref[0] = (o / l).astype(BF16)


def _mla_attn_ctx(Q, K, V, O, nlat, seq_ctx):
    M = seq_ctx
    blk = lambda w: pl.BlockSpec((1, M, w), lambda b, h: (nlat, b, h))
    return pl.pallas_call(
        _mla_attn_ctx_kernel,
        grid=(nlat, HEADS),
        in_specs=[blk(MLA_HEAD_PAD), blk(MLA_HEAD_PAD), blk(HEAD_DIM), pl.BlockSpec(memory_space=pl.ANY)],
        out_specs=blk(HEAD_DIM),
        out_shape=jax.ShapeDtypeStruct(O.shape, O.dtype),
        input_output_aliases={3: 0},
        compiler_params=_cparams(("parallel", "parallel")),
        name="mla_attn_ctx",
    )(Q, K, V, O)


def _head_norm_gate(o, gate, w):
    parts = []
    for h in range(HEADS):
        hs = slice(h * HEAD_DIM, (h + 1) * HEAD_DIM)
        parts.append(_rms(o[:, hs], w[:, hs]))
    return (jnp.concatenate(parts, axis=1) * _silu(gate)).astype(BF16)


def _merge_kernel(x_ref, mod_ref, gf_ref, gb_ref, z_ref, rf_ref, rb_ref, rg_ref, mo_ref, ga_ref, gbg_ref, gc_ref,
                  gnw_ref, rnw_ref, wa_ref, wb_ref, wc_ref, wo_ref, n2_ref, x1_ref, h2_ref):
    m = mod_ref[0]
    oa = _head_norm_gate(gf_ref[0] + gb_ref[0], z_ref[0], gnw_ref[...])
    oc = _head_norm_gate(rf_ref[0] + rb_ref[0], rg_ref[0], rnw_ref[...])
    ya = jnp.dot(oa, wa_ref[...], preferred_element_type=F32)
    yb = jnp.dot(mo_ref[0], wb_ref[...], preferred_element_type=F32)
    yc = jnp.dot(oc, wc_ref[...], preferred_element_type=F32)
    mix = _sigmoid(ga_ref[0]) * ya + _sigmoid(gbg_ref[0]) * yb + _sigmoid(gc_ref[0]) * yc
    y = jnp.dot(mix.astype(BF16), wo_ref[...], preferred_element_type=F32)
    x1 = x_ref[0] + m[2:3] * y
    x1_ref[0] = x1
    h2_ref[0] = (_rms(x1, n2_ref[...]) * (1.0 + m[4:5]) + m[3:4]).astype(BF16)


def _merge(x9, mod9, gdn_f, gdn_b, ret_f, ret_b, mla_o, P, gdn_nw, ret_nw, wa, wb, wc, wo, n2, nslab):
    S9, N, D = x9.shape
    tm = min(256, N)
    W = HEADS * HEAD_DIM
    row = lambda w, c=0: pl.BlockSpec((1, tm, w), lambda b, i, c=c: (b, i, c))
    full = lambda a: pl.BlockSpec(a.shape, lambda b, i: (0,) * a.ndim)
    g0 = P_GATES // D
    gnw = jnp.tile(gdn_nw, HEADS).reshape(1, W)
    rnw = ret_nw.reshape(1, W)
    n2 = n2.reshape(1, D)
    return pl.pallas_call(
        _merge_kernel,
        grid=(nslab, N // tm),
        in_specs=[row(D), pl.BlockSpec((1, 6, D), lambda b, i: (b, 0, 0)),
                  row(W), row(W), row(W, P_Z // W),
                  row(W), row(W), row(W, P_RET // W + 3),
                  row(W),
                  row(D, g0), row(D, g0 + 1), row(D, g0 + 2),
                  full(gnw), full(rnw), full(wa), full(wb), full(wc), full(wo), full(n2)],
        out_specs=[row(D), row(D)],
        out_shape=[jax.ShapeDtypeStruct((S9, N, D), F32), jax.ShapeDtypeStruct((S9, N, D), BF16)],
        compiler_params=_cparams(("parallel", "parallel")),
        name="merge",
    )(x9, mod9, gdn_f, gdn_b, P, ret_f, ret_b, P, mla_o, P, P, P, gnw, rnw, wa, wb, wc, wo, n2)


def _ffn_kernel(h_ref, hp_ref, hn_ref, x_ref, mod_ref, wg_ref, wv_ref, cwg_ref, cwv_ref, bg_ref, bv_ref, wd_ref,
                nw_ref, nmod_ref, *out_refs, nlat, seq_lat, seq_ctx, final):
    acc_ref = out_refs[-1]
    b, i, j = pl.program_id(0), pl.program_id(1), pl.program_id(2)
    tm = h_ref.shape[1]

    @pl.when(j == 0)
    def _():
        acc_ref[...] = jnp.zeros_like(acc_ref)

    hm = h_ref[0]
    hh = jnp.concatenate([hp_ref[0], hn_ref[0]], axis=0)
    seq_len = jnp.where(b == nlat, seq_ctx, seq_lat)

    def up(w_ref, cw_ref, b_ref):
        w = w_ref[...]
        u = jnp.dot(hm, w, preferred_element_type=F32)
        uh = jnp.dot(hh, w, preferred_element_type=F32)
        return _conv3_rows(u, uh, cw_ref[...], i * tm, seq_len) + b_ref[...]

    act = _silu(up(wg_ref, cwg_ref, bg_ref)) * up(wv_ref, cwv_ref, bv_ref)
    acc_ref[...] += jnp.dot(act.astype(BF16), wd_ref[...], preferred_element_type=F32)

    @pl.when(j == pl.num_programs(2) - 1)
    def _():
        m = mod_ref[0]
        x2 = x_ref[0] + m[5:6] * acc_ref[...]
        if final:
            out_refs[0][0] = _rms(x2, nw_ref[...])
        else:
            nm = nmod_ref[0]
            out_refs[0][0] = x2
            out_refs[1][0] = (_rms(x2, nw_ref[...]) * (1.0 + nm[1:2]) + nm[0:1]).astype(BF16)


def _ffn(h2, x1, mod9, w_up, conv_w, conv_b, w_down, next_nw, next_mod9, nslab, nlat, seq_ctx, final):
    S9, N, D = x1.shape
    FF = w_down.shape[0]
    tm = min(1024, N)
    tn = 256
    nj = FF // tn
    row = lambda: pl.BlockSpec((1, tm, D), lambda b, i, j: (b, i, 0))
    modspec = pl.BlockSpec((1, 6, D), lambda b, i, j: (b, 0, 0))
    cb = conv_b.reshape(1, 2 * FF)
    nw = next_nw.reshape(1, D)
    if final:
        out_specs = [row()]
        out_shape = [jax.ShapeDtypeStruct((nslab, N, D), F32)]
    else:
        out_specs = [row(), row()]
        out_shape = [jax.ShapeDtypeStruct((S9, N, D), F32), jax.ShapeDtypeStruct((S9, N, D), BF16)]
    return pl.pallas_call(
        functools.partial(_ffn_kernel, nlat=nlat, seq_lat=N, seq_ctx=seq_ctx, final=final),
        grid=(nslab, N // tm, nj),
        in_specs=_halo_specs(tm, N, D) + [
            row(), modspec,
            pl.BlockSpec((D, tn), lambda b, i, j: (0, j)),
            pl.BlockSpec((D, tn), lambda b, i, j: (0, j + nj)),
            pl.BlockSpec((3, tn), lambda b, i, j: (0, j)),
            pl.BlockSpec((3, tn), lambda b, i, j: (0, j + nj)),
            pl.BlockSpec((1, tn), lambda b, i, j: (0, j)),
            pl.BlockSpec((1, tn), lambda b, i, j: (0, j + nj)),
            pl.BlockSpec((tn, D), lambda b, i, j: (j, 0)),
            pl.BlockSpec((1, D), lambda b, i, j: (0, 0)),
            modspec],
        out_specs=out_specs,
        out_shape=out_shape,
        scratch_shapes=[pltpu.VMEM((tm, D), F32)],
        compiler_params=_cparams(("parallel", "parallel", "arbitrary")),
        name="ffn_final" if final else "ffn",
    )(h2, h2, h2, x1, mod9, w_up, w_up, conv_w, conv_w, cb, cb, w_down, nw, next_mod9)


def _axial_angles(n, d):
    rows = n // GRID_W
    r = jnp.repeat(jnp.arange(rows, dtype=F32), GRID_W)
    col = jnp.tile(jnp.arange(GRID_W, dtype=F32), rows)
    quarter = d // 4
    inv = ROPE_BASE ** (-jnp.arange(quarter, dtype=F32) / quarter)
    return jnp.concatenate([r[:, None] * inv, col[:, None] * inv], axis=-1)


def _rope_tables(n):
    a_r = _axial_angles(n, HEAD_DIM)
    cr, sr = jnp.cos(a_r), jnp.sin(a_r)
    cos_ret = jnp.stack([jnp.concatenate([cr, cr], -1), jnp.ones((n, HEAD_DIM), F32)])
    sin_ret = jnp.stack([jnp.concatenate([-sr, sr], -1), jnp.zeros((n, HEAD_DIM), F32)])
    a_m = _axial_angles(n, MLA_ROPE)
    cm, sm = jnp.cos(a_m), jnp.sin(a_m)
    cos_k = jnp.concatenate([cm, cm], -1)
    sin_k = jnp.concatenate([sm, sm], -1)
    pad = MLA_HEAD_PAD - MLA_NOPE - MLA_ROPE
    cos_q = jnp.concatenate([jnp.ones((n, MLA_NOPE), F32), cos_k, jnp.zeros((n, pad), F32)], -1)
    sin_q = jnp.concatenate([jnp.zeros((n, MLA_NOPE), F32), sin_k, jnp.zeros((n, pad), F32)], -1)
    ident_q = jnp.concatenate([jnp.ones((n, MLA_NOPE + MLA_ROPE), F32), jnp.zeros((n, pad), F32)], -1)
    return (cos_ret, sin_ret,
            jnp.stack([cos_q, ident_q]), jnp.stack([sin_q, jnp.zeros_like(sin_q)]),
            jnp.stack([cos_k, jnp.ones_like(cos_k)]), jnp.stack([sin_k, jnp.zeros_like(sin_k)]))


def _rot_half_cols(w):
    half = w.shape[-1] // 2
    return jnp.concatenate([-w[..., half:], w[..., :half]], axis=-1)


def kernel(x, c, ctx, c_ctx, ada_w, ada_b, norm1_w, w_in, gdn_conv_w, gdn_A_log, gdn_dt_bias, gdn_norm_w, mla_q_norm, mla_w_uq, mla_kv_norm, mla_w_ukv, ret_norm_w, w_br_gdn, w_br_mla, w_br_ret, w_out, norm2_w, ffn_w_up, ffn_conv_w, ffn_conv_b, ffn_w_down, final_norm_w):
    B, N, D = x.shape
    M = ctx.shape[1]
    L = ada_w.shape[0]
    W = HEADS * HEAD_DIM
    assert B * M == N and N % SEQ_BLOCK == 0 and M % SEQ_BLOCK == 0
    assert N & (N - 1) == 0 and M & (M - 1) == 0 and D == 1024

    x9 = jnp.concatenate([x, ctx.reshape(1, N, D)], axis=0)
    cc = jnp.zeros((16, D), F32).at[:B].set(c).at[B].set(c_ctx)
    mod = _modulation(cc, ada_w.astype(BF16), ada_b)[:, :B + 1].reshape(L, B + 1, 6, D)
    cos_ret, sin_ret, cos_q, sin_q, cos_k, sin_k = _rope_tables(N)

    h = _norm_mod(x9, mod[0], norm1_w[0])
    out = None
    for l in range(L):
        last = l == L - 1
        nslab = B if last else B + 1
        wi = w_in[l]
        o_qkv, o_z, o_ab = 3 * W, 4 * W, 4 * W + 16
        o_cq = o_ab
        o_ckv = o_cq + MLA_Q_RANK
        o_kr = o_ckv + MLA_KV_RANK
        o_ret = o_kr + MLA_ROPE
        o_gate = o_ret + 4 * W
        w_kr = wi[:, o_kr:o_ret]
        zc = lambda n: jnp.zeros((D, n), F32)
        w_p = jnp.concatenate([
            wi[:, o_cq:o_kr], w_kr, zc(64), _rot_half_cols(w_kr), zc(64), wi[:, o_z:o_ab], zc(112),
            wi[:, o_gate:], wi[:, o_ret:o_gate], wi[:, o_qkv:o_z]], axis=1).astype(BF16)
        assert w_p.shape[1] == P_COLS

        qkv = _gdn_proj(h, wi[:, :o_qkv].astype(BF16), gdn_conv_w[l], B, M)
        P = _proj_rest(h, w_p)
        abT = jnp.swapaxes(P[:, :, P_AB:P_AB + 16], 1, 2)
        gdn_f, gdn_b = _gdn_scan(qkv, P, abT, gdn_A_log[l], gdn_dt_bias[l], B, M)
        ret_f, ret_b = _ret_scan(P, cos_ret, sin_ret, B, M)

        wq = mla_w_uq[l].reshape(MLA_Q_RANK, HEADS, MLA_NOPE + MLA_ROPE)
        zq = jnp.zeros((MLA_Q_RANK, HEADS, MLA_HEAD_PAD - MLA_NOPE - MLA_ROPE), F32)
        wq_p = jnp.concatenate([wq, zq], -1).reshape(MLA_Q_RANK, HEADS * MLA_HEAD_PAD).astype(BF16)
        wq_r = jnp.concatenate([jnp.zeros_like(wq[..., :MLA_NOPE]), _rot_half_cols(wq[..., MLA_NOPE:]), zq],
                               -1).reshape(MLA_Q_RANK, HEADS * MLA_HEAD_PAD).astype(BF16)
        wkv = mla_w_ukv[l].reshape(MLA_KV_RANK, HEADS, 2 * HEAD_DIM)
        wk = wkv[..., :MLA_NOPE].reshape(MLA_KV_RANK, W).astype(BF16)
        wv = wkv[..., MLA_NOPE:].reshape(MLA_KV_RANK, W).astype(BF16)
        Q, K, V = _mla_prep(P, mla_q_norm[l], wq_p, wq_r, mla_kv_norm[l], wk, wv, cos_q, sin_q, cos_k, sin_k, B)
        mla_o = _mla_attn(Q, K, V, B, M)
        if not last:
            mla_o = _mla_attn_ctx(Q, K, V, mla_o, B, M)

        x1, h2 = _merge(x9, mod[l], gdn_f, gdn_b, ret_f, ret_b, mla_o, P, gdn_norm_w[l], ret_norm_w[l],
                        w_br_gdn[l].astype(BF16), w_br_mla[l].astype(BF16), w_br_ret[l].astype(BF16),
                        w_out[l].astype(BF16), norm2_w[l], nslab)
        ffn_args = (h2, x1, mod[l], ffn_w_up[l].astype(BF16), ffn_conv_w[l], ffn_conv_b[l],
                    ffn_w_down[l].astype(BF16))
        if last:
            (out,) = _ffn(*ffn_args, final_norm_w, mod[l], nslab, B, M, True)
        else:
            x9, h = _ffn(*ffn_args, norm1_w[l + 1], mod[l + 1], nslab, B, M, False)
    return out
```

```python
import functools
import math

import jax
import jax.numpy as jnp
from jax import lax
from jax.experimental import pallas as pl
from jax.experimental.pallas import tpu as pltpu

F32 = jnp.float32
BF16 = jnp.bfloat16
HIGHEST = lax.Precision.HIGHEST

EPS = 1e-6
GRID_W = 64
ROPE_BASE = 10000.0
HEADS = 4
HEAD_DIM = 128
CHUNK = 64
MLA_Q_RANK = 384
MLA_KV_RANK = 256
MLA_NOPE = 128
MLA_ROPE = 64
MLA_SCALE = (MLA_NOPE + MLA_ROPE) ** -0.5
MLA_HEAD_PAD = 256
RET_DECAY_BASE = 5.0
RET_DIR_OFFSET = 0.5
QK_SCALE = HEAD_DIM ** -0.5
NEG_BIG = -1e30

P_MLA = 0
P_KR = 640
P_KR_ROT = 768
P_AB = 896
P_GATES = 1024
P_RET = 4096
P_Z = 6144
P_COLS = 6656

V7X_VMEM_LIMIT = 56 * 1024 * 1024
SEQ_BLOCK = 256
HALO = 16


def _cparams(sem):
    return pltpu.CompilerParams(dimension_semantics=sem, vmem_limit_bytes=V7X_VMEM_LIMIT)


def _bdot(a, b):
    return jnp.dot(a.astype(BF16), b.astype(BF16), preferred_element_type=F32)


def _bdot_nt(a, b):
    return lax.dot_general(a.astype(BF16), b.astype(BF16), (((1,), (1,)), ((), ())),
                           preferred_element_type=F32)


def _bdot_tn(a, b):
    return lax.dot_general(a.astype(BF16), b.astype(BF16), (((0,), (0,)), ((), ())),
                           preferred_element_type=F32)


def _rms(x, w):
    return x * lax.rsqrt(jnp.mean(x * x, axis=-1, keepdims=True) + EPS) * w


def _silu(x):
    return x * (1.0 / (1.0 + jnp.exp(-x)))


def _sigmoid(x):
    return 1.0 / (1.0 + jnp.exp(-x))


def _mod_kernel(c_ref, w_ref, b_ref, o_ref):
    a = _silu(c_ref[...])
    o_ref[0] = _bdot(a, w_ref[0]) + b_ref[0]


def _modulation(cc, ada_w, ada_b):
    L, D, D6 = ada_w.shape
    R = cc.shape[0]
    tn = 1536
    return pl.pallas_call(
        _mod_kernel,
        grid=(L, D6 // tn),
        in_specs=[pl.BlockSpec((R, D), lambda l, j: (0, 0)),
                  pl.BlockSpec((1, D, tn), lambda l, j: (l, 0, j)),
                  pl.BlockSpec((1, 1, tn), lambda l, j: (l, 0, j))],
        out_specs=pl.BlockSpec((1, R, tn), lambda l, j: (l, 0, j)),
        out_shape=jax.ShapeDtypeStruct((L, R, D6), F32),
        compiler_params=_cparams(("parallel", "parallel")),
        name="modulation",
    )(cc, ada_w, ada_b.reshape(L, 1, D6))


def _norm_mod_kernel(x_ref, mod_ref, nw_ref, h_ref):
    m = mod_ref[0]
    h = _rms(x_ref[0], nw_ref[...]) * (1.0 + m[1:2]) + m[0:1]
    h_ref[0] = h.astype(BF16)


def _norm_mod(x9, mod9, nw):
    S, N, D = x9.shape
    tm = min(512, N)
    return pl.pallas_call(
        _norm_mod_kernel,
        grid=(S, N // tm),
        in_specs=[pl.BlockSpec((1, tm, D), lambda b, i: (b, i, 0)),
                  pl.BlockSpec((1, 6, D), lambda b, i: (b, 0, 0)),
                  pl.BlockSpec((1, D), lambda b, i: (0, 0))],
        out_specs=pl.BlockSpec((1, tm, D), lambda b, i: (b, i, 0)),
        out_shape=jax.ShapeDtypeStruct((S, N, D), BF16),
        compiler_params=_cparams(("parallel", "parallel")),
        name="norm_mod",
    )(x9, mod9, nw.reshape(1, D))


def _rows_with_halo(h_ref, hp_ref, hn_ref, row0, seq_len):
    tm = h_ref.shape[1]
    hp, hn = hp_ref[0], hn_ref[0]
    hp = jnp.where((row0 & (seq_len - 1)) == 0, jnp.zeros_like(hp), hp)
    hn = jnp.where(((row0 + tm) & (seq_len - 1)) == 0, jnp.zeros_like(hn), hn)
    return jnp.concatenate([hp, h_ref[0], hn], axis=0)


def _conv3_ext(ue, cw):
    n = ue.shape[0]
    mid = slice(HALO, n - HALO)
    return cw[0:1] * pltpu.roll(ue, 1, 0)[mid] + cw[1:2] * ue[mid] + cw[2:3] * pltpu.roll(ue, n - 1, 0)[mid]


def _halo_specs(tm, N, D, slab0):
    nb = tm // HALO
    last = N // HALO - 1
    return [pl.BlockSpec((1, tm, D), lambda b, i, *_: (b + slab0, i, 0)),
            pl.BlockSpec((1, HALO, D), lambda b, i, *_: (b + slab0, jnp.maximum(i * nb - 1, 0), 0)),
            pl.BlockSpec((1, HALO, D), lambda b, i, *_: (b + slab0, jnp.minimum((i + 1) * nb, last), 0))]


def _gdn_proj_kernel(h_ref, hp_ref, hn_ref, w_ref, cw_ref, *rest, seq_len):
    o_ref = rest[-1]
    i, j = pl.program_id(1), pl.program_id(2)
    tm = h_ref.shape[1]
    he = _rows_with_halo(h_ref, hp_ref, hn_ref, i * tm, seq_len)
    ue = jnp.dot(he, w_ref[...], preferred_element_type=F32)
    y = _silu(_conv3_ext(ue, cw_ref[...]))
    qk_scale = jnp.where(j == 0, QK_SCALE, 1.0)
    for h in range(HEADS):
        yh = y[:, h * HEAD_DIM:(h + 1) * HEAD_DIM]
        r = lax.rsqrt(jnp.sum(yh * yh, axis=-1, keepdims=True) + EPS)
        f = jnp.where(j == 2, 1.0, r * qk_scale)
        o_ref[0, :, h * HEAD_DIM:(h + 1) * HEAD_DIM] = yh * f


def _gdn_proj(h9, w_qkv, conv_w, slab0, nslab, seq_len, prev=None):
    S, N, D = h9.shape
    C = w_qkv.shape[1]
    tm = min(1024, seq_len)
    tn = HEADS * HEAD_DIM
    alias = [] if prev is None else [prev]
    return pl.pallas_call(
        functools.partial(_gdn_proj_kernel, seq_len=seq_len),
        grid=(nslab, N // tm, C // tn),
        in_specs=_halo_specs(tm, N, D, slab0) + [
            pl.BlockSpec((D, tn), lambda b, i, j: (0, j)),
            pl.BlockSpec((3, tn), lambda b, i, j: (0, j))] + [pl.BlockSpec(memory_space=pl.ANY)] * len(alias),
        out_specs=pl.BlockSpec((1, tm, tn), lambda b, i, j: (b + slab0, i, j)),
        out_shape=jax.ShapeDtypeStruct((S, N, C), F32),
        input_output_aliases={5: 0} if alias else {},
        compiler_params=_cparams(("parallel", "parallel", "arbitrary")),
        name="gdn_proj",
    )(h9, h9, h9, w_qkv, conv_w, *alias)


def _matmul_kernel(h_ref, w_ref, o_ref):
    o_ref[0] = jnp.dot(h_ref[0], w_ref[...], preferred_element_type=F32)


def _proj_rest(h9, w_p):
    S, N, D = h9.shape
    C = w_p.shape[1]
    tm = min(1024, N)
    tn = 512
    return pl.pallas_call(
        _matmul_kernel,
        grid=(S, N // tm, C // tn),
        in_specs=[pl.BlockSpec((1, tm, D), lambda b, i, j: (b, i, 0)),
                  pl.BlockSpec((D, tn), lambda b, i, j: (0, j))],
        out_specs=pl.BlockSpec((1, tm, tn), lambda b, i, j: (b, i, j)),
        out_shape=jax.ShapeDtypeStruct((S, N, C), F32),
        compiler_params=_cparams(("parallel", "parallel", "arbitrary")),
        name="proj_rest",
    )(h9, w_p)


def _seq_maps(nlat, n_ctx_blk, n_lat_blk):
    def fwd(b, s):
        is_ctx = s < n_ctx_blk
        return jnp.where(is_ctx, nlat, b), jnp.where(is_ctx, b * n_ctx_blk + s, s - n_ctx_blk)

    def bwd(b, s):
        is_ctx = s < n_ctx_blk
        return (jnp.where(is_ctx, nlat, b),
                jnp.where(is_ctx, b * n_ctx_blk + n_ctx_blk - 1 - s, n_lat_blk - 1 - (s - n_ctx_blk)))

    return fwd, bwd


def _tri_masks(n, backward):
    r = lax.broadcasted_iota(jnp.int32, (n, n), 0)
    c = lax.broadcasted_iota(jnp.int32, (n, n), 1)
    if backward:
        return r <= c, r < c
    return r >= c, r > c


def _gdn_gates(backward, ab_ref, abt_ref, alog_c, dtb_c, alog_r, dtb_r):
    ab = ab_ref[0]
    abt = abt_ref[0]
    T = ab.shape[0]

    def softplus(x):
        return jnp.maximum(x, 0.0) + jnp.log1p(jnp.exp(-jnp.abs(x)))

    g_c = -jnp.exp(alog_c) * softplus(ab + dtb_c)
    g_r = -jnp.exp(alog_r) * softplus(abt + dtb_r)
    beta_c = _sigmoid(ab)
    ri = lax.broadcasted_iota(jnp.int32, (T, T), 0)
    ci = lax.broadcasted_iota(jnp.int32, (T, T), 1)
    same = (ri // CHUNK) == (ci // CHUNK)
    cs = jnp.where(same & ((ci >= ri) if backward else (ci <= ri)), 1.0, 0.0)
    gcum_c = jnp.dot(cs, g_c, precision=HIGHEST, preferred_element_type=F32)
    gcum_r = lax.dot_general(g_r, cs, (((1,), (1,)), ((), ())), precision=HIGHEST,
                             preferred_element_type=F32)
    return gcum_c, gcum_r, jnp.exp(gcum_c), beta_c


def _gdn_kernel(qf, kf, vf, abf, abtf, qb, kb, vb, abb, abtb, alog_c, dtb_c, alog_r, dtb_r,
                of_ref, ob_ref, sf_ref, sb_ref):
    @pl.when(pl.program_id(1) == 0)
    def _():
        sf_ref[...] = jnp.zeros_like(sf_ref)
        sb_ref[...] = jnp.zeros_like(sb_ref)

    ac, dc, ar, dr = alog_c[...], dtb_c[...], alog_r[...], dtb_r[...]
    C = CHUNK
    nchunk = qf.shape[1] // C
    eye = jnp.where(lax.broadcasted_iota(jnp.int32, (C, C), 0)
                    == lax.broadcasted_iota(jnp.int32, (C, C), 1), 1.0, 0.0)
    dirs = []
    for d, (q_ref, k_ref, v_ref, ab_ref, abt_ref, o_ref, s_ref) in enumerate(
            ((qf, kf, vf, abf, abtf, of_ref, sf_ref), (qb, kb, vb, abb, abtb, ob_ref, sb_ref))):
        backward = d == 1
        gates = _gdn_gates(backward, ab_ref, abt_ref, ac, dc, ar, dr)
        dirs.append(dict(d=d, backward=backward, q_ref=q_ref, k_ref=k_ref, v_ref=v_ref, o_ref=o_ref,
                         s_ref=s_ref, gates=gates, masks=_tri_masks(C, backward)))

    units = []
    for pos in range(nchunk):
        for dr_ in dirs:
            c = nchunk - 1 - pos if dr_["backward"] else pos
            for h in range(HEADS):
                units.append(dict(dr_, h=h, c=c, pos=pos))

    for u in units:
        gcum_c, gcum_r, eg_c, beta_c = u["gates"]
        incl, strict = u["masks"]
        d, h, c = u["d"], u["h"], u["c"]
        ia, ib = 4 * d + h, 8 + 4 * d + h
        rs = slice(c * C, (c + 1) * C)
        hs = slice(h * HEAD_DIM, (h + 1) * HEAD_DIM)
        u["rs"], u["hs"] = rs, hs
        q = u["q_ref"][0, rs, hs]
        k = u["k_ref"][0, rs, hs]
        v = u["v_ref"][0, rs, hs]
        gc = gcum_c[rs, ia:ia + 1]
        gr = gcum_r[ia:ia + 1, rs]
        egc = eg_c[rs, ia:ia + 1]
        beta = beta_c[rs, ib:ib + 1]
        last = c * C if u["backward"] else (c + 1) * C - 1
        gtot = gcum_c[last:last + 1, ia:ia + 1]
        decay = jnp.exp(jnp.where(incl, gc - gr, NEG_BIG))
        kbeta = k * beta
        qk = _bdot_nt(jnp.concatenate([q, kbeta], axis=0), k)
        u["attn"] = (qk[:C] * decay).astype(BF16)
        x = -(qk[C:] * jnp.where(strict, decay, 0.0))
        u["p"] = x
        u["t"] = eye + x
        u["rhs"] = jnp.concatenate([v * beta, kbeta * egc], axis=1).astype(BF16)
        u["qd"] = (q * egc).astype(BF16)
        u["kd"] = (k * jnp.exp(gtot - gc)).astype(BF16)
        u["cdec"] = jnp.exp(gtot)

    for u in units:
        u["p"] = _bdot(u["p"], u["p"])
    for _ in range(int(math.log2(C)) - 2):
        for u in units:
            r = _bdot(jnp.concatenate([u["p"], u["t"]], axis=0), u["p"])
            u["t"] = u["t"] + r[C:]
            u["p"] = r[:C]
    for u in units:
        u["t"] = u["t"] + _bdot(u["t"], u["p"])

    for u in units:
        u["sol"] = _bdot(u["t"], u["rhs"])

    S = {(dr_["d"], h): dr_["s_ref"][h] for dr_ in dirs for h in range(HEADS)}
    for pos in range(nchunk):
        cur = [u for u in units if u["pos"] == pos]
        for u in cur:
            lhs = jnp.concatenate([u["sol"][:, HEAD_DIM:].astype(BF16), u["qd"]], axis=0)
            u["ws"] = _bdot(lhs, S[u["d"], u["h"]])
        for u in cur:
            key = (u["d"], u["h"])
            v_new = (u["sol"][:, :HEAD_DIM] - u["ws"][:C]).astype(BF16)
            u["o_ref"][0, u["rs"], u["hs"]] = u["ws"][C:] + _bdot(u["attn"], v_new)
            S[key] = u["cdec"] * S[key] + _bdot_tn(u["kd"], v_new)
    for dr_ in dirs:
        for h in range(HEADS):
            dr_["s_ref"][h] = S[dr_["d"], h]


def _gdn_scan(qkv, P, abT, A_log, dt_bias, nlat, seq_ctx):
    S9, N, _ = qkv.shape
    T = SEQ_BLOCK
    ncb, nlb = seq_ctx // T, N // T
    fwd, bwd = _seq_maps(nlat, ncb, nlb)
    W = HEADS * HEAD_DIM

    def specs(m):
        return [pl.BlockSpec((1, T, W), lambda b, s, m=m: (*m(b, s), 0)),
                pl.BlockSpec((1, T, W), lambda b, s, m=m: (*m(b, s), 1)),
                pl.BlockSpec((1, T, W), lambda b, s, m=m: (*m(b, s), 2)),
                pl.BlockSpec((1, T, 128), lambda b, s, m=m: (*m(b, s), P_AB // 128)),
                pl.BlockSpec((1, 16, T), lambda b, s, m=m: (m(b, s)[0], 0, m(b, s)[1]))]

    small = lambda shape: pl.BlockSpec(shape, lambda b, s: (0, 0))
    alog = A_log.reshape(-1)
    dtb = dt_bias.reshape(-1)
    pad_c = lambda t: jnp.zeros((1, 128), F32).at[0, :8].set(t)
    pad_r = lambda t: jnp.zeros((16, 1), F32).at[:8, 0].set(t)
    out_sds = jax.ShapeDtypeStruct((S9, N, W), F32)
    return pl.pallas_call(
        _gdn_kernel,
        grid=(nlat, ncb + nlb),
        in_specs=specs(fwd) + specs(bwd) + [small((1, 128)), small((1, 128)), small((16, 1)), small((16, 1))],
        out_specs=[pl.BlockSpec((1, T, W), lambda b, s: (*fwd(b, s), 0)),
                   pl.BlockSpec((1, T, W), lambda b, s: (*bwd(b, s), 0))],
        out_shape=[out_sds, out_sds],
        scratch_shapes=[pltpu.VMEM((HEADS, HEAD_DIM, HEAD_DIM), F32),
                        pltpu.VMEM((HEADS, HEAD_DIM, HEAD_DIM), F32)],
        compiler_params=_cparams(("parallel", "arbitrary")),
        name="gdn_scan",
    )(qkv, qkv, qkv, P, abT, qkv, qkv, qkv, P, abT, pad_c(alog), pad_c(dtb), pad_r(alog), pad_r(dtb))


def _ret_log_gamma(d, h):
    return math.log1p(-(2.0 ** (-(RET_DECAY_BASE + h + RET_DIR_OFFSET * d))))


def _ret_kernel(qf, kf, vf, cf, sf, qb, kb, vb, cb, sb, of_ref, ob_ref, stf_ref, stb_ref):
    @pl.when(pl.program_id(1) == 0)
    def _():
        stf_ref[...] = jnp.zeros_like(stf_ref)
        stb_ref[...] = jnp.zeros_like(stb_ref)

    C = qf.shape[1]
    r = lax.broadcasted_iota(jnp.int32, (C, C), 0)
    c = lax.broadcasted_iota(jnp.int32, (C, C), 1)
    dist = jnp.abs(r - c).astype(F32)
    pos = lax.broadcasted_iota(jnp.int32, (C, 1), 0).astype(F32)
    half = HEAD_DIM // 2
    units = []
    for d, (q_ref, k_ref, v_ref, cos_ref, sin_ref, o_ref, s_ref) in enumerate(
            ((qf, kf, vf, cf, sf, of_ref, stf_ref), (qb, kb, vb, cb, sb, ob_ref, stb_ref))):
        backward = d == 1
        incl, _ = _tri_masks(C, backward)
        cos, sin = cos_ref[0], sin_ref[0]
        trav = (C - 1 - pos) if backward else pos
        for h in range(HEADS):
            lg = _ret_log_gamma(d, h)
            hs = slice(h * HEAD_DIM, (h + 1) * HEAD_DIM)
            q = q_ref[0, :, hs]
            k = k_ref[0, :, hs]
            q = (q * cos + pltpu.roll(q, half, 1) * sin) * QK_SCALE
            k = k * cos + pltpu.roll(k, half, 1) * sin
            v = v_ref[0, :, hs].astype(BF16)
            decay = jnp.where(incl, jnp.exp(lg * dist), 0.0)
            S = s_ref[h]
            units.append(dict(
                o_ref=o_ref, s_ref=s_ref, h=h, hs=hs, v=v, decay=decay,
                qk=_bdot_nt(q, k),
                inter=_bdot(q * jnp.exp(lg * (trav + 1.0)), S),
                s_new=math.exp(lg * C) * S + _bdot_tn(k * jnp.exp(lg * (C - 1 - trav)), v)))
    for u in units:
        a = (u["qk"] * u["decay"]).astype(BF16)
        u["o_ref"][0, :, u["hs"]] = _bdot(a, u["v"]) + u["inter"]
        u["s_ref"][u["h"]] = u["s_new"]


def _ret_scan(P, cos2, sin2, nlat, seq_ctx):
    S9, N, _ = P.shape
    T = SEQ_BLOCK
    ncb, nlb = seq_ctx // T, N // T
    fwd, bwd = _seq_maps(nlat, ncb, nlb)
    W = HEADS * HEAD_DIM
    c0 = P_RET // W

    def tab(m):
        def f(b, s):
            slab, blk = m(b, s)
            is_ctx = slab == nlat
            return jnp.where(is_ctx, 1, 0), jnp.where(is_ctx, 0, blk), 0
        return f

    def specs(m):
        return [pl.BlockSpec((1, T, W), lambda b, s, m=m: (*m(b, s), c0)),
                pl.BlockSpec((1, T, W), lambda b, s, m=m: (*m(b, s), c0 + 1)),
                pl.BlockSpec((1, T, W), lambda b, s, m=m: (*m(b, s), c0 + 2)),
                pl.BlockSpec((1, T, HEAD_DIM), tab(m)),
                pl.BlockSpec((1, T, HEAD_DIM), tab(m))]

    out_sds = jax.ShapeDtypeStruct((S9, N, W), F32)
    return pl.pallas_call(
        _ret_kernel,
        grid=(nlat, ncb + nlb),
        in_specs=specs(fwd) + specs(bwd),
        out_specs=[pl.BlockSpec((1, T, W), lambda b, s: (*fwd(b, s), 0)),
                   pl.BlockSpec((1, T, W), lambda b, s: (*bwd(b, s), 0))],
        out_shape=[out_sds, out_sds],
        scratch_shapes=[pltpu.VMEM((HEADS, HEAD_DIM, HEAD_DIM), F32),
                        pltpu.VMEM((HEADS, HEAD_DIM, HEAD_DIM), F32)],
        compiler_params=_cparams(("parallel", "arbitrary")),
        name="ret_scan",
    )(P, P, P, cos2, sin2, P, P, P, cos2, sin2)


def _mla_prep_kernel(p_ref, qn_ref, wq_ref, wqr_ref, kvn_ref, wk_ref, wv_ref, cq_ref, sq_ref, ck_ref, sk_ref,
                     q_ref, k_ref, v_ref):
    p = p_ref[0]
    cq = _rms(p[:, :MLA_Q_RANK], qn_ref[...]).astype(BF16)
    ckv = _rms(p[:, MLA_Q_RANK:MLA_Q_RANK + MLA_KV_RANK], kvn_ref[...]).astype(BF16)
    cosq, sinq = cq_ref[0], sq_ref[0]
    q = jnp.dot(cq, wq_ref[...], preferred_element_type=F32)
    qr = jnp.dot(cq, wqr_ref[...], preferred_element_type=F32)
    for h in range(HEADS):
        hs = slice(h * MLA_HEAD_PAD, (h + 1) * MLA_HEAD_PAD)
        q_ref[0, :, hs] = (q[:, hs] * cosq + qr[:, hs] * sinq).astype(BF16)
    kn = jnp.dot(ckv, wk_ref[...], preferred_element_type=F32)
    v_ref[0] = jnp.dot(ckv, wv_ref[...], preferred_element_type=F32).astype(BF16)
    krope = (p[:, P_KR:P_KR + MLA_ROPE] * ck_ref[0] + p[:, P_KR_ROT:P_KR_ROT + MLA_ROPE] * sk_ref[0]).astype(BF16)
    zpad = jnp.zeros((p.shape[0], MLA_HEAD_PAD - MLA_NOPE - MLA_ROPE), BF16)
    for h in range(HEADS):
        o = h * MLA_HEAD_PAD
        k_ref[0, :, o:o + MLA_NOPE] = kn[:, h * MLA_NOPE:(h + 1) * MLA_NOPE].astype(BF16)
        k_ref[0, :, o + MLA_NOPE:o + MLA_NOPE + MLA_ROPE] = krope
        k_ref[0, :, o + MLA_NOPE + MLA_ROPE:o + MLA_HEAD_PAD] = zpad


def _mla_prep(P, q_norm, wq, wq_rot, kv_norm, wk, wv, cosq, sinq, cosk, sink, nlat):
    S9, N, _ = P.shape
    tm = min(512, N)
    QW = HEADS * MLA_HEAD_PAD
    VW = HEADS * HEAD_DIM
    full = lambda a: pl.BlockSpec(a.shape, lambda b, i: (0,) * a.ndim)
    tab = lambda w: pl.BlockSpec((1, tm, w), lambda b, i: (jnp.where(b == nlat, 1, 0), jnp.where(b == nlat, 0, i), 0))
    qn = q_norm.reshape(1, -1)
    kvn = kv_norm.reshape(1, -1)
    return pl.pallas_call(
        _mla_prep_kernel,
        grid=(S9, N // tm),
        in_specs=[pl.BlockSpec((1, tm, 1024), lambda b, i: (b, i, P_MLA // 1024)),
                  full(qn), full(wq), full(wq_rot), full(kvn), full(wk), full(wv),
                  tab(MLA_HEAD_PAD), tab(MLA_HEAD_PAD), tab(MLA_ROPE), tab(MLA_ROPE)],
        out_specs=[pl.BlockSpec((1, tm, QW), lambda b, i: (b, i, 0)),
                   pl.BlockSpec((1, tm, QW), lambda b, i: (b, i, 0)),
                   pl.BlockSpec((1, tm, VW), lambda b, i: (b, i, 0))],
        out_shape=[jax.ShapeDtypeStruct((S9, N, QW), BF16),
                   jax.ShapeDtypeStruct((S9, N, QW), BF16),
                   jax.ShapeDtypeStruct((S9, N, VW), BF16)],
        compiler_params=_cparams(("parallel", "parallel")),
        name="mla_prep",
    )(P, qn, wq, wq_rot, kvn, wk, wv, cosq, sinq, cosk, sink)


MLA_SCALE_LOG2E = MLA_SCALE * math.log2(math.e)


def _mla_attn_kernel(q_ref, kl_ref, kc_ref, vl_ref, vc_ref, o_ref):
    nt = lambda a, b: lax.dot_general(a, b, (((1,), (1,)), ((), ())), preferred_element_type=F32)

    def scores(h):
        hs = slice(h * MLA_HEAD_PAD, (h + 1) * MLA_HEAD_PAD)
        q = q_ref[0, :, hs]
        return nt(q, kl_ref[0, :, hs]), nt(q, kc_ref[0, :, hs])

    def probs(s1, s2):
        m = jnp.maximum(jnp.max(s1, axis=-1, keepdims=True), jnp.max(s2, axis=-1, keepdims=True))
        mc = m * MLA_SCALE_LOG2E
        p1 = jnp.exp2(s1 * MLA_SCALE_LOG2E - mc)
        p2 = jnp.exp2(s2 * MLA_SCALE_LOG2E - mc)
        l = jnp.sum(p1, axis=-1, keepdims=True) + jnp.sum(p2, axis=-1, keepdims=True)
        return p1.astype(BF16), p2.astype(BF16), l

    def out(h, p1, p2, l):
        hs = slice(h * HEAD_DIM, (h + 1) * HEAD_DIM)
        o = (jnp.dot(p1, vl_ref[0, :, hs], preferred_element_type=F32)
             + jnp.dot(p2, vc_ref[0, :, hs], preferred_element_type=F32))
        o_ref[0, :, hs] = (o / l).astype(BF16)

    s, p = {}, {}
    for step in range(HEADS + 2):
        if step < HEADS:
            s[step] = scores(step)
        if 0 <= step - 1 < HEADS:
            p[step - 1] = probs(*s.pop(step - 1))
        if 0 <= step - 2 < HEADS:
            out(step - 2, *p.pop(step - 2))


def _mla_attn(Q, K, V, nlat, seq_ctx):
    S9, N, QW = Q.shape
    VW = V.shape[2]
    tq = min(256, N)
    M = seq_ctx
    return pl.pallas_call(
        _mla_attn_kernel,
        grid=(nlat, N // tq),
        in_specs=[pl.BlockSpec((1, tq, QW), lambda b, i: (b, i, 0)),
                  pl.BlockSpec((1, N, QW), lambda b, i: (b, 0, 0)),
                  pl.BlockSpec((1, M, QW), lambda b, i: (nlat, b, 0)),
                  pl.BlockSpec((1, N, VW), lambda b, i: (b, 0, 0)),
                  pl.BlockSpec((1, M, VW), lambda b, i: (nlat, b, 0))],
        out_specs=pl.BlockSpec((1, tq, VW), lambda b, i: (b, i, 0)),
        out_shape=jax.ShapeDtypeStruct((S9, N, VW), BF16),
        compiler_params=_cparams(("parallel", "arbitrary")),
        name="mla_attn",
    )(Q, K, K, V, V)


def _mla_attn_ctx_kernel(q_ref, kc_ref, vc_ref, o_in_ref, o_ref):
    del o_in_ref
    s = lax.dot_general(q_ref[0], kc_ref[0], (((1,), (1,)), ((), ())), preferred_element_type=F32)
    m = jnp.max(s, axis=-1, keepdims=True)
    p = jnp.exp((s - m) * MLA_SCALE)
    l = jnp.sum(p, axis=-1, keepdims=True)
    o = jnp.dot(p.astype(BF16), vc_ref[0], preferred_element_type=F32)
    o_ref[0] = (o / l).astype(BF16)


def _mla_attn_ctx(Q, K, V, O, nlat, seq_ctx):
    M = seq_ctx
    blk = lambda w: pl.BlockSpec((1, M, w), lambda b, h: (nlat, b, h))
    return pl.pallas_call(
        _mla_attn_ctx_kernel,
        grid=(nlat, HEADS),
        in_specs=[blk(MLA_HEAD_PAD), blk(MLA_HEAD_PAD), blk(HEAD_DIM), pl.BlockSpec(memory_space=pl.ANY)],
        out_specs=blk(HEAD_DIM),
        out_shape=jax.ShapeDtypeStruct(O.shape, O.dtype),
        input_output_aliases={3: 0},
        compiler_params=_cparams(("parallel", "parallel")),
        name="mla_attn_ctx",
    )(Q, K, V, O)


def _head_norm_gate(o, gate, w):
    parts = []
    for h in range(HEADS):
        hs = slice(h * HEAD_DIM, (h + 1) * HEAD_DIM)
        parts.append(_rms(o[:, hs], w[:, hs]))
    return (jnp.concatenate(parts, axis=1) * _silu(gate)).astype(BF16)


def _merge_kernel(x_ref, mod_ref, gf_ref, gb_ref, z_ref, rf_ref, rb_ref, rg_ref, mo_ref, ga_ref, gbg_ref, gc_ref,
                  gnw_ref, rnw_ref, wa_ref, wb_ref, wc_ref, wo_ref, n2_ref, x1_ref, h2_ref):
    m = mod_ref[0]
    oa = _head_norm_gate(gf_ref[0] + gb_ref[0], z_ref[0], gnw_ref[...])
    oc = _head_norm_gate(rf_ref[0] + rb_ref[0], rg_ref[0], rnw_ref[...])
    ya = jnp.dot(oa, wa_ref[...], preferred_element_type=F32)
    yb = jnp.dot(mo_ref[0], wb_ref[...], preferred_element_type=F32)
    yc = jnp.dot(oc, wc_ref[...], preferred_element_type=F32)
    mix = _sigmoid(ga_ref[0]) * ya + _sigmoid(gbg_ref[0]) * yb + _sigmoid(gc_ref[0]) * yc
    y = jnp.dot(mix.astype(BF16), wo_ref[...], preferred_element_type=F32)
    x1 = x_ref[0] + m[2:3] * y
    x1_ref[0] = x1
    h2_ref[0] = (_rms(x1, n2_ref[...]) * (1.0 + m[4:5]) + m[3:4]).astype(BF16)


def _merge(x9, mod9, gdn_f, gdn_b, ret_f, ret_b, mla_o, P, gdn_nw, ret_nw, wa, wb, wc, wo, n2, nslab):
    S9, N, D = x9.shape
    tm = min(256, N)
    W = HEADS * HEAD_DIM
    row = lambda w, c=0: pl.BlockSpec((1, tm, w), lambda b, i, c=c: (b, i, c))
    full = lambda a: pl.BlockSpec(a.shape, lambda b, i: (0,) * a.ndim)
    g0 = P_GATES // D
    gnw = jnp.tile(gdn_nw, HEADS).reshape(1, W)
    rnw = ret_nw.reshape(1, W)
    n2 = n2.reshape(1, D)
    return pl.pallas_call(
        _merge_kernel,
        grid=(nslab, N // tm),
        in_specs=[row(D), pl.BlockSpec((1, 6, D), lambda b, i: (b, 0, 0)),
                  row(W), row(W), row(W, P_Z // W),
                  row(W), row(W), row(W, P_RET // W + 3),
                  row(W),
                  row(D, g0), row(D, g0 + 1), row(D, g0 + 2),
                  full(gnw), full(rnw), full(wa), full(wb), full(wc), full(wo), full(n2)],
        out_specs=[row(D), row(D)],
        out_shape=[jax.ShapeDtypeStruct((S9, N, D), F32), jax.ShapeDtypeStruct((S9, N, D), BF16)],
        compiler_params=_cparams(("parallel", "parallel")),
        name="merge",
    )(x9, mod9, gdn_f, gdn_b, P, ret_f, ret_b, P, mla_o, P, P, P, gnw, rnw, wa, wb, wc, wo, n2)


FFN_CHUNK = 256


def _ffn_kernel(h_ref, hp_ref, hn_ref, x_ref, mod_ref, wup_ref, cw_ref, cb_ref, wd_ref, nw_ref, nmod_ref,
                *rest, seq_len, final):
    out_refs = rest[-1:] if final else rest[-2:]
    i = pl.program_id(1)
    tm = h_ref.shape[1]
    FF = wd_ref.shape[0]
    nch = FF // FFN_CHUNK
    he = _rows_with_halo(h_ref, hp_ref, hn_ref, i * tm, seq_len)

    def up(c):
        g = slice(c * FFN_CHUNK, (c + 1) * FFN_CHUNK)
        v = slice(FF + c * FFN_CHUNK, FF + (c + 1) * FFN_CHUNK)
        return (jnp.dot(he, wup_ref[:, g], preferred_element_type=F32),
                jnp.dot(he, wup_ref[:, v], preferred_element_type=F32))

    def act(c, ug, uv):
        g = slice(c * FFN_CHUNK, (c + 1) * FFN_CHUNK)
        v = slice(FF + c * FFN_CHUNK, FF + (c + 1) * FFN_CHUNK)
        gate = _conv3_ext(ug, cw_ref[:, g]) + cb_ref[:, g]
        val = _conv3_ext(uv, cw_ref[:, v]) + cb_ref[:, v]
        return (_silu(gate) * val).astype(BF16)

    nxt = up(0)
    acc = None
    for c in range(nch):
        cur = nxt
        if c + 1 < nch:
            nxt = up(c + 1)
        part = jnp.dot(act(c, *cur), wd_ref[c * FFN_CHUNK:(c + 1) * FFN_CHUNK, :], preferred_element_type=F32)
        acc = part if acc is None else acc + part

    m = mod_ref[0]
    x2 = x_ref[0] + m[5:6] * acc
    if final:
        out_refs[0][0] = _rms(x2, nw_ref[...])
    else:
        nm = nmod_ref[0]
        out_refs[0][0] = x2
        out_refs[1][0] = (_rms(x2, nw_ref[...]) * (1.0 + nm[1:2]) + nm[0:1]).astype(BF16)


def _ffn(h2, x1, mod9, w_up, conv_w, conv_b, w_down, next_nw, next_mod9, slab0, nslab, seq_len, final, prev=()):
    S9, N, D = x1.shape
    FF = w_down.shape[0]
    tm = min(512, seq_len)
    row = lambda: pl.BlockSpec((1, tm, D), lambda b, i: (b + slab0, i, 0))
    modspec = pl.BlockSpec((1, 6, D), lambda b, i: (b + slab0, 0, 0))
    resident = lambda a: pl.BlockSpec(a.shape, lambda b, i: (0,) * a.ndim, pipeline_mode=pl.Buffered(1))
    cb = conv_b.reshape(1, 2 * FF)
    nw = next_nw.reshape(1, D)
    if final:
        out_specs = [row()]
        out_shape = [jax.ShapeDtypeStruct((nslab, N, D), F32)]
    else:
        out_specs = [row(), row()]
        out_shape = [jax.ShapeDtypeStruct((S9, N, D), F32), jax.ShapeDtypeStruct((S9, N, D), BF16)]
    n_in = 11
    return pl.pallas_call(
        functools.partial(_ffn_kernel, seq_len=seq_len, final=final),
        grid=(nslab, N // tm),
        in_specs=_halo_specs(tm, N, D, slab0) + [
            row(), modspec, resident(w_up), resident(conv_w), resident(cb), resident(w_down),
            pl.BlockSpec((1, D), lambda b, i: (0, 0)), modspec] + [pl.BlockSpec(memory_space=pl.ANY)] * len(prev),
        out_specs=out_specs,
        out_shape=out_shape,
        input_output_aliases={n_in + k: k for k in range(len(prev))},
        compiler_params=_cparams(("parallel", "parallel")),
        name="ffn_final" if final else "ffn",
    )(h2, h2, h2, x1, mod9, w_up, conv_w, cb, w_down, nw, next_mod9, *prev)


def _axial_angles(n, d):
    rows = n // GRID_W
    r = jnp.repeat(jnp.arange(rows, dtype=F32), GRID_W)
    col = jnp.tile(jnp.arange(GRID_W, dtype=F32), rows)
    quarter = d // 4
    inv = ROPE_BASE ** (-jnp.arange(quarter, dtype=F32) / quarter)
    return jnp.concatenate([r[:, None] * inv, col[:, None] * inv], axis=-1)


def _rope_tables(n):
    a_r = _axial_angles(n, HEAD_DIM)
    cr, sr = jnp.cos(a_r), jnp.sin(a_r)
    cos_ret = jnp.stack([jnp.concatenate([cr, cr], -1), jnp.ones((n, HEAD_DIM), F32)])
    sin_ret = jnp.stack([jnp.concatenate([-sr, sr], -1), jnp.zeros((n, HEAD_DIM), F32)])
    a_m = _axial_angles(n, MLA_ROPE)
    cm, sm = jnp.cos(a_m), jnp.sin(a_m)
    cos_k = jnp.concatenate([cm, cm], -1)
    sin_k = jnp.concatenate([sm, sm], -1)
    pad = MLA_HEAD_PAD - MLA_NOPE - MLA_ROPE
    cos_q = jnp.concatenate([jnp.ones((n, MLA_NOPE), F32), cos_k, jnp.zeros((n, pad), F32)], -1)
    sin_q = jnp.concatenate([jnp.zeros((n, MLA_NOPE), F32), sin_k, jnp.zeros((n, pad), F32)], -1)
    ident_q = jnp.concatenate([jnp.ones((n, MLA_NOPE + MLA_ROPE), F32), jnp.zeros((n, pad), F32)], -1)
    return (cos_ret, sin_ret,
            jnp.stack([cos_q, ident_q]), jnp.stack([sin_q, jnp.zeros_like(sin_q)]),
            jnp.stack([cos_k, jnp.ones_like(cos_k)]), jnp.stack([sin_k, jnp.zeros_like(sin_k)]))


def _rot_half_cols(w):
    half = w.shape[-1] // 2
    return jnp.concatenate([-w[..., half:], w[..., :half]], axis=-1)


def kernel(x, c, ctx, c_ctx, ada_w, ada_b, norm1_w, w_in, gdn_conv_w, gdn_A_log, gdn_dt_bias, gdn_norm_w, mla_q_norm, mla_w_uq, mla_kv_norm, mla_w_ukv, ret_norm_w, w_br_gdn, w_br_mla, w_br_ret, w_out, norm2_w, ffn_w_up, ffn_conv_w, ffn_conv_b, ffn_w_down, final_norm_w):
    B, N, D = x.shape
    M = ctx.shape[1]
    L = ada_w.shape[0]
    W = HEADS * HEAD_DIM
    assert B * M == N and N % SEQ_BLOCK == 0 and M % SEQ_BLOCK == 0
    assert N & (N - 1) == 0 and M & (M - 1) == 0 and D == 1024

    x9 = jnp.concatenate([x, ctx.reshape(1, N, D)], axis=0)
    cc = jnp.zeros((16, D), F32).at[:B].set(c).at[B].set(c_ctx)
    mod = _modulation(cc, ada_w.astype(BF16), ada_b)[:, :B + 1].reshape(L, B + 1, 6, D)
    cos_ret, sin_ret, cos_q, sin_q, cos_k, sin_k = _rope_tables(N)

    h = _norm_mod(x9, mod[0], norm1_w[0])
    out = None
    for l in range(L):
        last = l == L - 1
        nslab = B if last else B + 1
        wi = w_in[l]
        o_qkv, o_z, o_ab = 3 * W, 4 * W, 4 * W + 16
        o_cq = o_ab
        o_ckv = o_cq + MLA_Q_RANK
        o_kr = o_ckv + MLA_KV_RANK
        o_ret = o_kr + MLA_ROPE
        o_gate = o_ret + 4 * W
        w_kr = wi[:, o_kr:o_ret]
        zc = lambda n: jnp.zeros((D, n), F32)
        w_p = jnp.concatenate([
            wi[:, o_cq:o_kr], w_kr, zc(64), _rot_half_cols(w_kr), zc(64), wi[:, o_z:o_ab], zc(112),
            wi[:, o_gate:], wi[:, o_ret:o_gate], wi[:, o_qkv:o_z]], axis=1).astype(BF16)
        assert w_p.shape[1] == P_COLS

        w_qkv = wi[:, :o_qkv].astype(BF16)
        qkv = _gdn_proj(h, w_qkv, gdn_conv_w[l], 0, B, N)
        qkv = _gdn_proj(h, w_qkv, gdn_conv_w[l], B, 1, M, prev=qkv)
        P = _proj_rest(h, w_p)
        abT = jnp.swapaxes(P[:, :, P_AB:P_AB + 16], 1, 2)
        gdn_f, gdn_b = _gdn_scan(qkv, P, abT, gdn_A_log[l], gdn_dt_bias[l], B, M)
        ret_f, ret_b = _ret_scan(P, cos_ret, sin_ret, B, M)

        wq = mla_w_uq[l].reshape(MLA_Q_RANK, HEADS, MLA_NOPE + MLA_ROPE)
        zq = jnp.zeros((MLA_Q_RANK, HEADS, MLA_HEAD_PAD - MLA_NOPE - MLA_ROPE), F32)
        wq_p = jnp.concatenate([wq, zq], -1).reshape(MLA_Q_RANK, HEADS * MLA_HEAD_PAD).astype(BF16)
        wq_r = jnp.concatenate([jnp.zeros_like(wq[..., :MLA_NOPE]), _rot_half_cols(wq[..., MLA_NOPE:]), zq],
                               -1).reshape(MLA_Q_RANK, HEADS * MLA_HEAD_PAD).astype(BF16)
        wkv = mla_w_ukv[l].reshape(MLA_KV_RANK, HEADS, 2 * HEAD_DIM)
        wk = wkv[..., :MLA_NOPE].reshape(MLA_KV_RANK, W).astype(BF16)
        wv = wkv[..., MLA_NOPE:].reshape(MLA_KV_RANK, W).astype(BF16)
        Q, K, V = _mla_prep(P, mla_q_norm[l], wq_p, wq_r, mla_kv_norm[l], wk, wv, cos_q, sin_q, cos_k, sin_k, B)
        mla_o = _mla_attn(Q, K, V, B, M)
        if not last:
            mla_o = _mla_attn_ctx(Q, K, V, mla_o, B, M)

        x1, h2 = _merge(x9, mod[l], gdn_f, gdn_b, ret_f, ret_b, mla_o, P, gdn_norm_w[l], ret_norm_w[l],
                        w_br_gdn[l].astype(BF16), w_br_mla[l].astype(BF16), w_br_ret[l].astype(BF16),
                        w_out[l].astype(BF16), norm2_w[l], nslab)
        ffn_args = (h2, x1, mod[l], ffn_w_up[l].astype(BF16), ffn_conv_w[l], ffn_conv_b[l],
                    ffn_w_down[l].astype(BF16))
        if last:
            (out,) = _ffn(*ffn_args, final_norm_w, mod[l], 0, B, N, True)
        else:
            nxt = _ffn(*ffn_args, norm1_w[l + 1], mod[l + 1], 0, B, N, False)
            x9, h = _ffn(*ffn_args, norm1_w[l + 1], mod[l + 1], B, 1, M, False, prev=tuple(nxt))
    return out
```

```python
import functools
import math

import jax
import jax.numpy as jnp
from jax import lax
from jax.experimental import pallas as pl
from jax.experimental.pallas import tpu as pltpu

F32 = jnp.float32
BF16 = jnp.bfloat16
HIGHEST = lax.Precision.HIGHEST

EPS = 1e-6
GRID_W = 64
ROPE_BASE = 10000.0
HEADS = 4
HEAD_DIM = 128
CHUNK = 64
MLA_Q_RANK = 384
MLA_KV_RANK = 256
MLA_NOPE = 128
MLA_ROPE = 64
MLA_SCALE = (MLA_NOPE + MLA_ROPE) ** -0.5
MLA_HEAD_PAD = 256
RET_DECAY_BASE = 5.0
RET_DIR_OFFSET = 0.5
QK_SCALE = HEAD_DIM ** -0.5
NEG_BIG = -1e30

P_MLA = 0
P_KR = 640
P_KR_ROT = 768
P_AB = 896
P_GATES = 1024
P_RET = 4096
P_Z = 6144
P_COLS = 6656

V7X_VMEM_LIMIT = 56 * 1024 * 1024
SEQ_BLOCK = 256
HALO = 16


def _cparams(sem):
    return pltpu.CompilerParams(dimension_semantics=sem, vmem_limit_bytes=V7X_VMEM_LIMIT)


def _bdot(a, b):
    return jnp.dot(a.astype(BF16), b.astype(BF16), preferred_element_type=F32)


def _bdot_nt(a, b):
    return lax.dot_general(a.astype(BF16), b.astype(BF16), (((1,), (1,)), ((), ())),
                           preferred_element_type=F32)


def _bdot_tn(a, b):
    return lax.dot_general(a.astype(BF16), b.astype(BF16), (((0,), (0,)), ((), ())),
                           preferred_element_type=F32)


def _rms(x, w):
    return x * lax.rsqrt(jnp.mean(x * x, axis=-1, keepdims=True) + EPS) * w


def _silu(x):
    return x * (1.0 / (1.0 + jnp.exp(-x)))


def _sigmoid(x):
    return 1.0 / (1.0 + jnp.exp(-x))


def _mod_kernel(c_ref, w_ref, b_ref, o_ref):
    a = _silu(c_ref[...])
    o_ref[0] = _bdot(a, w_ref[0]) + b_ref[0]


def _modulation(cc, ada_w, ada_b):
    L, D, D6 = ada_w.shape
    R = cc.shape[0]
    tn = 1536
    return pl.pallas_call(
        _mod_kernel,
        grid=(L, D6 // tn),
        in_specs=[pl.BlockSpec((R, D), lambda l, j: (0, 0)),
                  pl.BlockSpec((1, D, tn), lambda l, j: (l, 0, j)),
                  pl.BlockSpec((1, 1, tn), lambda l, j: (l, 0, j))],
        out_specs=pl.BlockSpec((1, R, tn), lambda l, j: (l, 0, j)),
        out_shape=jax.ShapeDtypeStruct((L, R, D6), F32),
        compiler_params=_cparams(("parallel", "parallel")),
        name="modulation",
    )(cc, ada_w, ada_b.reshape(L, 1, D6))


def _norm_mod_kernel(x_ref, mod_ref, nw_ref, h_ref):
    m = mod_ref[0]
    h = _rms(x_ref[0], nw_ref[...]) * (1.0 + m[1:2]) + m[0:1]
    h_ref[0] = h.astype(BF16)


def _norm_mod(x9, mod9, nw):
    S, N, D = x9.shape
    tm = min(512, N)
    return pl.pallas_call(
        _norm_mod_kernel,
        grid=(S, N // tm),
        in_specs=[pl.BlockSpec((1, tm, D), lambda b, i: (b, i, 0)),
                  pl.BlockSpec((1, 6, D), lambda b, i: (b, 0, 0)),
                  pl.BlockSpec((1, D), lambda b, i: (0, 0))],
        out_specs=pl.BlockSpec((1, tm, D), lambda b, i: (b, i, 0)),
        out_shape=jax.ShapeDtypeStruct((S, N, D), BF16),
        compiler_params=_cparams(("parallel", "parallel")),
        name="norm_mod",
    )(x9, mod9, nw.reshape(1, D))


def _rows_with_halo(h_ref, hp_ref, hn_ref, row0, seq_len):
    tm = h_ref.shape[1]
    hp, hn = hp_ref[0], hn_ref[0]
    hp = jnp.where((row0 & (seq_len - 1)) == 0, jnp.zeros_like(hp), hp)
    hn = jnp.where(((row0 + tm) & (seq_len - 1)) == 0, jnp.zeros_like(hn), hn)
    return jnp.concatenate([hp, h_ref[0], hn], axis=0)


def _conv3_ext(ue, cw, short_len, is_short):
    n = ue.shape[0]
    tm = n - 2 * HALO
    mid = slice(HALO, n - HALO)
    y = cw[0:1] * pltpu.roll(ue, 1, 0)[mid] + cw[1:2] * ue[mid] + cw[2:3] * pltpu.roll(ue, n - 1, 0)[mid]
    if short_len >= tm:
        return y
    rows = lax.broadcasted_iota(jnp.int32, (2 * 8, 1), 0)
    pieces, at = [], 0
    for edge in range(short_len, tm, short_len):
        blk = ue[HALO + edge - 8:HALO + edge + 8]
        fix = (jnp.where(rows == 7, is_short, 0.0) * (cw[2:3] * blk[8:9])
               + jnp.where(rows == 8, is_short, 0.0) * (cw[0:1] * blk[7:8]))
        pieces += [y[at:edge - 8], y[edge - 8:edge + 8] - fix]
        at = edge + 8
    return jnp.concatenate(pieces + [y[at:]], axis=0)


def _halo_specs(tm, N, D):
    nb = tm // HALO
    last = N // HALO - 1
    return [pl.BlockSpec((1, tm, D), lambda b, i, *_: (b, i, 0)),
            pl.BlockSpec((1, HALO, D), lambda b, i, *_: (b, jnp.maximum(i * nb - 1, 0), 0)),
            pl.BlockSpec((1, HALO, D), lambda b, i, *_: (b, jnp.minimum((i + 1) * nb, last), 0))]


def _slab_seq(b, nlat, seq_lat, seq_ctx):
    is_ctx = b == nlat
    return jnp.where(is_ctx, seq_ctx, seq_lat), jnp.where(is_ctx, 1.0, 0.0)


def _gdn_proj_kernel(h_ref, hp_ref, hn_ref, w_ref, cw_ref, o_ref, *, nlat, seq_lat, seq_ctx):
    b, i, j = pl.program_id(0), pl.program_id(1), pl.program_id(2)
    tm = h_ref.shape[1]
    seq_len, is_ctx = _slab_seq(b, nlat, seq_lat, seq_ctx)
    he = _rows_with_halo(h_ref, hp_ref, hn_ref, i * tm, seq_len)
    ue = jnp.dot(he, w_ref[...], preferred_element_type=F32)
    y = _silu(_conv3_ext(ue, cw_ref[...], seq_ctx, is_ctx))
    qk_scale = jnp.where(j == 0, QK_SCALE, 1.0)
    for h in range(HEADS):
        yh = y[:, h * HEAD_DIM:(h + 1) * HEAD_DIM]
        r = lax.rsqrt(jnp.sum(yh * yh, axis=-1, keepdims=True) + EPS)
        f = jnp.where(j == 2, 1.0, r * qk_scale)
        o_ref[0, :, h * HEAD_DIM:(h + 1) * HEAD_DIM] = yh * f


def _gdn_proj(h9, w_qkv, conv_w, nlat, seq_ctx):
    S, N, D = h9.shape
    C = w_qkv.shape[1]
    tm = min(1024, N)
    tn = HEADS * HEAD_DIM
    return pl.pallas_call(
        functools.partial(_gdn_proj_kernel, nlat=nlat, seq_lat=N, seq_ctx=seq_ctx),
        grid=(S, N // tm, C // tn),
        in_specs=_halo_specs(tm, N, D) + [
            pl.BlockSpec((D, tn), lambda b, i, j: (0, j)),
            pl.BlockSpec((3, tn), lambda b, i, j: (0, j))],
        out_specs=pl.BlockSpec((1, tm, tn), lambda b, i, j: (b, i, j)),
        out_shape=jax.ShapeDtypeStruct((S, N, C), F32),
        compiler_params=_cparams(("parallel", "parallel", "arbitrary")),
        name="gdn_proj",
    )(h9, h9, h9, w_qkv, conv_w)


def _ret_proj_kernel(h_ref, w_ref, cos_ref, sin_ref, o_ref):
    j = pl.program_id(2)
    u = jnp.dot(h_ref[0], w_ref[...], preferred_element_type=F32)

    @pl.when(j < 2)
    def _():
        cos, sin = cos_ref[0], sin_ref[0]
        scale = jnp.where(j == 0, QK_SCALE, 1.0)
        for h in range(HEADS):
            hs = slice(h * HEAD_DIM, (h + 1) * HEAD_DIM)
            t = u[:, hs]
            o_ref[0, :, hs] = (t * cos + pltpu.roll(t, HEAD_DIM // 2, 1) * sin) * scale

    @pl.when(j == 2)
    def _():
        o_ref[0] = u


def _ret_proj(h9, w_ret, cos2, sin2, nlat):
    S, N, D = h9.shape
    C = w_ret.shape[1]
    tm = min(1024, N)
    tn = HEADS * HEAD_DIM
    tab = pl.BlockSpec((1, tm, HEAD_DIM), lambda b, i, j: (jnp.where(b == nlat, 1, 0), jnp.where(b == nlat, 0, i), 0))
    return pl.pallas_call(
        _ret_proj_kernel,
        grid=(S, N // tm, C // tn),
        in_specs=[pl.BlockSpec((1, tm, D), lambda b, i, j: (b, i, 0)),
                  pl.BlockSpec((D, tn), lambda b, i, j: (0, j)), tab, tab],
        out_specs=pl.BlockSpec((1, tm, tn), lambda b, i, j: (b, i, j)),
        out_shape=jax.ShapeDtypeStruct((S, N, C), F32),
        compiler_params=_cparams(("parallel", "parallel", "arbitrary")),
        name="ret_proj",
    )(h9, w_ret, cos2, sin2)


def _seq_maps(nlat, n_ctx_blk, n_lat_blk):
    def fwd(b, s):
        is_ctx = s < n_ctx_blk
        return jnp.where(is_ctx, nlat, b), jnp.where(is_ctx, b * n_ctx_blk + s, s - n_ctx_blk)

    def bwd(b, s):
        is_ctx = s < n_ctx_blk
        return (jnp.where(is_ctx, nlat, b),
                jnp.where(is_ctx, b * n_ctx_blk + n_ctx_blk - 1 - s, n_lat_blk - 1 - (s - n_ctx_blk)))

    return fwd, bwd


def _tri_masks(n, backward):
    r = lax.broadcasted_iota(jnp.int32, (n, n), 0)
    c = lax.broadcasted_iota(jnp.int32, (n, n), 1)
    if backward:
        return r <= c, r < c
    return r >= c, r > c


def _gdn_gates(backward, ab_ref, abt_ref, alog_c, dtb_c, alog_r, dtb_r):
    ab = ab_ref[0]
    abt = abt_ref[0]
    T = ab.shape[0]

    def softplus(x):
        return jnp.maximum(x, 0.0) + jnp.log1p(jnp.exp(-jnp.abs(x)))

    g_c = -jnp.exp(alog_c) * softplus(ab + dtb_c)
    g_r = -jnp.exp(alog_r) * softplus(abt + dtb_r)
    beta_c = _sigmoid(ab)
    ri = lax.broadcasted_iota(jnp.int32, (T, T), 0)
    ci = lax.broadcasted_iota(jnp.int32, (T, T), 1)
    same = (ri // CHUNK) == (ci // CHUNK)
    cs = jnp.where(same & ((ci >= ri) if backward else (ci <= ri)), 1.0, 0.0)
    gcum_c = jnp.dot(cs, g_c, precision=HIGHEST, preferred_element_type=F32)
    gcum_r = lax.dot_general(g_r, cs, (((1,), (1,)), ((), ())), precision=HIGHEST,
                             preferred_element_type=F32)
    return gcum_c, gcum_r, jnp.exp(gcum_c), beta_c


def _gdn_kernel(qf, kf, vf, abf, abtf, qb, kb, vb, abb, abtb, alog_c, dtb_c, alog_r, dtb_r,
                of_ref, ob_ref, sf_ref, sb_ref):
    @pl.when(pl.program_id(1) == 0)
    def _():
        sf_ref[...] = jnp.zeros_like(sf_ref)
        sb_ref[...] = jnp.zeros_like(sb_ref)

    ac, dc, ar, dr = alog_c[...], dtb_c[...], alog_r[...], dtb_r[...]
    C = CHUNK
    nchunk = qf.shape[1] // C
    eye = jnp.where(lax.broadcasted_iota(jnp.int32, (C, C), 0)
                    == lax.broadcasted_iota(jnp.int32, (C, C), 1), 1.0, 0.0)
    dirs = []
    for d, (q_ref, k_ref, v_ref, ab_ref, abt_ref, o_ref, s_ref) in enumerate(
            ((qf, kf, vf, abf, abtf, of_ref, sf_ref), (qb, kb, vb, abb, abtb, ob_ref, sb_ref))):
        backward = d == 1
        gates = _gdn_gates(backward, ab_ref, abt_ref, ac, dc, ar, dr)
        dirs.append(dict(d=d, backward=backward, q_ref=q_ref, k_ref=k_ref, v_ref=v_ref, o_ref=o_ref,
                         s_ref=s_ref, gates=gates, masks=_tri_masks(C, backward)))

    units = []
    for pos in range(nchunk):
        for dr_ in dirs:
            c = nchunk - 1 - pos if dr_["backward"] else pos
            for h in range(HEADS):
                units.append(dict(dr_, h=h, c=c, pos=pos))

    for u in units:
        gcum_c, gcum_r, eg_c, beta_c = u["gates"]
        incl, strict = u["masks"]
        d, h, c = u["d"], u["h"], u["c"]
        ia, ib = 4 * d + h, 8 + 4 * d + h
        rs = slice(c * C, (c + 1) * C)
        hs = slice(h * HEAD_DIM, (h + 1) * HEAD_DIM)
        u["rs"], u["hs"] = rs, hs
        q = u["q_ref"][0, rs, hs]
        k = u["k_ref"][0, rs, hs]
        v = u["v_ref"][0, rs, hs]
        gc = gcum_c[rs, ia:ia + 1]
        gr = gcum_r[ia:ia + 1, rs]
        egc = eg_c[rs, ia:ia + 1]
        beta = beta_c[rs, ib:ib + 1]
        last = c * C if u["backward"] else (c + 1) * C - 1
        gtot = gcum_c[last:last + 1, ia:ia + 1]
        decay = jnp.exp(jnp.where(incl, gc - gr, NEG_BIG))
        kbeta = k * beta
        qk = _bdot_nt(jnp.concatenate([q, kbeta], axis=0), k)
        u["attn"] = (qk[:C] * decay).astype(BF16)
        x = -(qk[C:] * jnp.where(strict, decay, 0.0))
        u["p"] = x
        u["t"] = eye + x
        u["rhs"] = jnp.concatenate([v * beta, kbeta * egc], axis=1).astype(BF16)
        u["qd"] = (q * egc).astype(BF16)
        u["kd"] = (k * jnp.exp(gtot - gc)).astype(BF16)
        u["cdec"] = jnp.exp(gtot)

    for u in units:
        u["p"] = _bdot(u["p"], u["p"])
    for _ in range(int(math.log2(C)) - 2):
        for u in units:
            r = _bdot(jnp.concatenate([u["p"], u["t"]], axis=0), u["p"])
            u["t"] = u["t"] + r[C:]
            u["p"] = r[:C]
    for u in units:
        u["t"] = u["t"] + _bdot(u["t"], u["p"])

    for u in units:
        u["sol"] = _bdot(u["t"], u["rhs"])

    S = {(dr_["d"], h): dr_["s_ref"][h] for dr_ in dirs for h in range(HEADS)}
    for pos in range(nchunk):
        cur = [u for u in units if u["pos"] == pos]
        for u in cur:
            lhs = jnp.concatenate([u["sol"][:, HEAD_DIM:].astype(BF16), u["qd"]], axis=0)
            u["ws"] = _bdot(lhs, S[u["d"], u["h"]])
        for u in cur:
            key = (u["d"], u["h"])
            v_new = (u["sol"][:, :HEAD_DIM] - u["ws"][:C]).astype(BF16)
            u["o_ref"][0, u["rs"], u["hs"]] = u["ws"][C:] + _bdot(u["attn"], v_new)
            S[key] = u["cdec"] * S[key] + _bdot_tn(u["kd"], v_new)
    for dr_ in dirs:
        for h in range(HEADS):
            dr_["s_ref"][h] = S[dr_["d"], h]


def _gdn_scan(qkv, ab, abT, A_log, dt_bias, nlat, seq_ctx):
    S9, N, _ = qkv.shape
    T = SEQ_BLOCK
    ncb, nlb = seq_ctx // T, N // T
    fwd, bwd = _seq_maps(nlat, ncb, nlb)
    W = HEADS * HEAD_DIM

    def specs(m):
        return [pl.BlockSpec((1, T, W), lambda b, s, m=m: (*m(b, s), 0)),
                pl.BlockSpec((1, T, W), lambda b, s, m=m: (*m(b, s), 1)),
                pl.BlockSpec((1, T, W), lambda b, s, m=m: (*m(b, s), 2)),
                pl.BlockSpec((1, T, 128), lambda b, s, m=m: (*m(b, s), 0)),
                pl.BlockSpec((1, 16, T), lambda b, s, m=m: (m(b, s)[0], 0, m(b, s)[1]))]

    small = lambda shape: pl.BlockSpec(shape, lambda b, s: (0, 0))
    alog = A_log.reshape(-1)
    dtb = dt_bias.reshape(-1)
    pad_c = lambda t: jnp.zeros((1, 128), F32).at[0, :8].set(t)
    pad_r = lambda t: jnp.zeros((16, 1), F32).at[:8, 0].set(t)
    out_sds = jax.ShapeDtypeStruct((S9, N, W), F32)
    return pl.pallas_call(
        _gdn_kernel,
        grid=(nlat, ncb + nlb),
        in_specs=specs(fwd) + specs(bwd) + [small((1, 128)), small((1, 128)), small((16, 1)), small((16, 1))],
        out_specs=[pl.BlockSpec((1, T, W), lambda b, s: (*fwd(b, s), 0)),
                   pl.BlockSpec((1, T, W), lambda b, s: (*bwd(b, s), 0))],
        out_shape=[out_sds, out_sds],
        scratch_shapes=[pltpu.VMEM((HEADS, HEAD_DIM, HEAD_DIM), F32),
                        pltpu.VMEM((HEADS, HEAD_DIM, HEAD_DIM), F32)],
        compiler_params=_cparams(("parallel", "arbitrary")),
        name="gdn_scan",
    )(qkv, qkv, qkv, ab, abT, qkv, qkv, qkv, ab, abT, pad_c(alog), pad_c(dtb), pad_r(alog), pad_r(dtb))


def _ret_log_gamma(d, h):
    return math.log1p(-(2.0 ** (-(RET_DECAY_BASE + h + RET_DIR_OFFSET * d))))


def _ret_kernel(qf, kf, vf, qb, kb, vb, of_ref, ob_ref, stf_ref, stb_ref):
    @pl.when(pl.program_id(1) == 0)
    def _():
        stf_ref[...] = jnp.zeros_like(stf_ref)
        stb_ref[...] = jnp.zeros_like(stb_ref)

    C = qf.shape[1]
    r = lax.broadcasted_iota(jnp.int32, (C, C), 0)
    c = lax.broadcasted_iota(jnp.int32, (C, C), 1)
    dist = jnp.abs(r - c).astype(F32)
    pos = lax.broadcasted_iota(jnp.int32, (C, 1), 0).astype(F32)
    units = []
    for d, (q_ref, k_ref, v_ref, o_ref, s_ref) in enumerate(
            ((qf, kf, vf, of_ref, stf_ref), (qb, kb, vb, ob_ref, stb_ref))):
        backward = d == 1
        incl, _ = _tri_masks(C, backward)
        trav = (C - 1 - pos) if backward else pos
        for h in range(HEADS):
            lg = _ret_log_gamma(d, h)
            hs = slice(h * HEAD_DIM, (h + 1) * HEAD_DIM)
            q = q_ref[0, :, hs]
            k = k_ref[0, :, hs]
            v = v_ref[0, :, hs].astype(BF16)
            decay = jnp.where(incl, jnp.exp(lg * dist), 0.0)
            S = s_ref[h]
            units.append(dict(
                o_ref=o_ref, s_ref=s_ref, h=h, hs=hs, v=v, decay=decay,
                qk=_bdot_nt(q, k),
                inter=_bdot(q * jnp.exp(lg * (trav + 1.0)), S),
                s_new=math.exp(lg * C) * S + _bdot_tn(k * jnp.exp(lg * (C - 1 - trav)), v)))
    for u in units:
        a = (u["qk"] * u["decay"]).astype(BF16)
        u["o_ref"][0, :, u["hs"]] = _bdot(a, u["v"]) + u["inter"]
        u["s_ref"][u["h"]] = u["s_new"]


def _ret_scan(qkv, nlat, seq_ctx):
    S9, N, _ = qkv.shape
    T = SEQ_BLOCK
    ncb, nlb = seq_ctx // T, N // T
    fwd, bwd = _seq_maps(nlat, ncb, nlb)
    W = HEADS * HEAD_DIM

    def specs(m):
        return [pl.BlockSpec((1, T, W), lambda b, s, m=m: (*m(b, s), 0)),
                pl.BlockSpec((1, T, W), lambda b, s, m=m: (*m(b, s), 1)),
                pl.BlockSpec((1, T, W), lambda b, s, m=m: (*m(b, s), 2))]

    out_sds = jax.ShapeDtypeStruct((S9, N, W), F32)
    return pl.pallas_call(
        _ret_kernel,
        grid=(nlat, ncb + nlb),
        in_specs=specs(fwd) + specs(bwd),
        out_specs=[pl.BlockSpec((1, T, W), lambda b, s: (*fwd(b, s), 0)),
                   pl.BlockSpec((1, T, W), lambda b, s: (*bwd(b, s), 0))],
        out_shape=[out_sds, out_sds],
        scratch_shapes=[pltpu.VMEM((HEADS, HEAD_DIM, HEAD_DIM), F32),
                        pltpu.VMEM((HEADS, HEAD_DIM, HEAD_DIM), F32)],
        compiler_params=_cparams(("parallel", "arbitrary")),
        name="ret_scan",
    )(qkv, qkv, qkv, qkv, qkv, qkv)


def _mla_prep_kernel(h_ref, wm_ref, wabt_ref, qn_ref, wq_ref, wqr_ref, kvn_ref, wk_ref, wv_ref,
                     cq_ref, sq_ref, ck_ref, sk_ref, q_ref, k_ref, v_ref, ab_ref, abt_ref):
    hrow = h_ref[0]
    p = jnp.dot(hrow, wm_ref[...], preferred_element_type=F32)
    ab_ref[0] = p[:, P_AB:P_AB + 128]
    abt_ref[0] = lax.dot_general(wabt_ref[...], hrow, (((1,), (1,)), ((), ())), preferred_element_type=F32)
    cq = _rms(p[:, :MLA_Q_RANK], qn_ref[...]).astype(BF16)
    ckv = _rms(p[:, MLA_Q_RANK:MLA_Q_RANK + MLA_KV_RANK], kvn_ref[...]).astype(BF16)
    cosq, sinq = cq_ref[0], sq_ref[0]
    q = jnp.dot(cq, wq_ref[...], preferred_element_type=F32)
    qr = jnp.dot(cq, wqr_ref[...], preferred_element_type=F32)
    for h in range(HEADS):
        hs = slice(h * MLA_HEAD_PAD, (h + 1) * MLA_HEAD_PAD)
        q_ref[0, :, hs] = (q[:, hs] * cosq + qr[:, hs] * sinq).astype(BF16)
    kn = jnp.dot(ckv, wk_ref[...], preferred_element_type=F32)
    v_ref[0] = jnp.dot(ckv, wv_ref[...], preferred_element_type=F32).astype(BF16)
    krope = (p[:, P_KR:P_KR + MLA_ROPE] * ck_ref[0] + p[:, P_KR_ROT:P_KR_ROT + MLA_ROPE] * sk_ref[0]).astype(BF16)
    zpad = jnp.zeros((p.shape[0], MLA_HEAD_PAD - MLA_NOPE - MLA_ROPE), BF16)
    for h in range(HEADS):
        o = h * MLA_HEAD_PAD
        k_ref[0, :, o:o + MLA_NOPE] = kn[:, h * MLA_NOPE:(h + 1) * MLA_NOPE].astype(BF16)
        k_ref[0, :, o + MLA_NOPE:o + MLA_NOPE + MLA_ROPE] = krope
        k_ref[0, :, o + MLA_NOPE + MLA_ROPE:o + MLA_HEAD_PAD] = zpad


def _mla_prep(h9, w_mla, w_abt, q_norm, wq, wq_rot, kv_norm, wk, wv, cosq, sinq, cosk, sink, nlat):
    S9, N, D = h9.shape
    tm = min(512, N)
    QW = HEADS * MLA_HEAD_PAD
    VW = HEADS * HEAD_DIM
    full = lambda a: pl.BlockSpec(a.shape, lambda b, i: (0,) * a.ndim)
    tab = lambda w: pl.BlockSpec((1, tm, w), lambda b, i: (jnp.where(b == nlat, 1, 0), jnp.where(b == nlat, 0, i), 0))
    row = lambda w: pl.BlockSpec((1, tm, w), lambda b, i: (b, i, 0))
    qn = q_norm.reshape(1, -1)
    kvn = kv_norm.reshape(1, -1)
    return pl.pallas_call(
        _mla_prep_kernel,
        grid=(S9, N // tm),
        in_specs=[row(D), full(w_mla), full(w_abt),
                  full(qn), full(wq), full(wq_rot), full(kvn), full(wk), full(wv),
                  tab(MLA_HEAD_PAD), tab(MLA_HEAD_PAD), tab(MLA_ROPE), tab(MLA_ROPE)],
        out_specs=[row(QW), row(QW), row(VW), row(128),
                   pl.BlockSpec((1, 16, tm), lambda b, i: (b, 0, i))],
        out_shape=[jax.ShapeDtypeStruct((S9, N, QW), BF16),
                   jax.ShapeDtypeStruct((S9, N, QW), BF16),
                   jax.ShapeDtypeStruct((S9, N, VW), BF16),
                   jax.ShapeDtypeStruct((S9, N, 128), F32),
                   jax.ShapeDtypeStruct((S9, 16, N), F32)],
        compiler_params=_cparams(("parallel", "parallel")),
        name="mla_prep",
    )(h9, w_mla, w_abt, qn, wq, wq_rot, kvn, wk, wv, cosq, sinq, cosk, sink)


MLA_SCALE_LOG2E = MLA_SCALE * math.log2(math.e)


def _mla_attn_kernel(q_ref, kl_ref, kc_ref, vl_ref, vc_ref, o_ref, *, n_lat_tiles):
    nt = lambda a, b: lax.dot_general(a, b, (((1,), (1,)), ((), ())), preferred_element_type=F32)

    def heads(latent_keys):
        def scores(h):
            hs = slice(h * MLA_HEAD_PAD, (h + 1) * MLA_HEAD_PAD)
            q = q_ref[0, :, hs]
            s2 = nt(q, kc_ref[0, :, hs])
            return (nt(q, kl_ref[0, :, hs]), s2) if latent_keys else (s2,)

        def probs(*ss):
            m = functools.reduce(jnp.maximum, [jnp.max(s, axis=-1, keepdims=True) for s in ss])
            mc = m * MLA_SCALE_LOG2E
            ps = [jnp.exp2(s * MLA_SCALE_LOG2E - mc) for s in ss]
            l = functools.reduce(jnp.add, [jnp.sum(p, axis=-1, keepdims=True) for p in ps])
            return [p.astype(BF16) for p in ps], l

        def out(h, ps, l):
            hs = slice(h * HEAD_DIM, (h + 1) * HEAD_DIM)
            o = jnp.dot(ps[-1], vc_ref[0, :, hs], preferred_element_type=F32)
            if latent_keys:
                o = o + jnp.dot(ps[0], vl_ref[0, :, hs], preferred_element_type=F32)
            o_ref[0, :, hs] = (o / l).astype(BF16)

        s, p = {}, {}
        for step in range(HEADS + 2):
            if step < HEADS:
                s[step] = scores(step)
            if 0 <= step - 1 < HEADS:
                p[step - 1] = probs(*s.pop(step - 1))
            if 0 <= step - 2 < HEADS:
                out(step - 2, *p.pop(step - 2))

    i = pl.program_id(1)
    pl.when(i < n_lat_tiles)(lambda: heads(True))
    pl.when(i >= n_lat_tiles)(lambda: heads(False))


def _mla_attn(Q, K, V, nlat, seq_ctx, with_ctx):
    S9, N, QW = Q.shape
    VW = V.shape[2]
    M = seq_ctx
    tq = M
    nq = N // tq

    def qmap(b, i):
        is_ctx = i >= nq
        return jnp.where(is_ctx, nlat, b), jnp.where(is_ctx, b, i), 0

    return pl.pallas_call(
        functools.partial(_mla_attn_kernel, n_lat_tiles=nq),
        grid=(nlat, nq + 1 if with_ctx else nq),
        in_specs=[pl.BlockSpec((1, tq, QW), qmap),
                  pl.BlockSpec((1, N, QW), lambda b, i: (b, 0, 0)),
                  pl.BlockSpec((1, M, QW), lambda b, i: (nlat, b, 0)),
                  pl.BlockSpec((1, N, VW), lambda b, i: (b, 0, 0)),
                  pl.BlockSpec((1, M, VW), lambda b, i: (nlat, b, 0))],
        out_specs=pl.BlockSpec((1, tq, VW), qmap),
        out_shape=jax.ShapeDtypeStruct((S9 if with_ctx else nlat, N, VW), BF16),
        compiler_params=_cparams(("parallel", "arbitrary")),
        name="mla_attn",
    )(Q, K, K, V, V)


def _head_norm_gate(o, gate, w):
    parts = []
    for h in range(HEADS):
        hs = slice(h * HEAD_DIM, (h + 1) * HEAD_DIM)
        parts.append(_rms(o[:, hs], w[:, hs]))
    return (jnp.concatenate(parts, axis=1) * _silu(gate)).astype(BF16)


def _sigmoid_tanh(x):
    return 0.5 * jnp.tanh(0.5 * x) + 0.5


def _merge_kernel(x_ref, mod_ref, h_ref, gf_ref, gb_ref, rf_ref, rb_ref, mo_ref,
                  gnw_ref, rnw_ref, wg_ref, wa_ref, wb_ref, wc_ref, wo_ref, n2_ref, x1_ref, h2_ref):
    m = mod_ref[0]
    D = x_ref.shape[2]
    W = HEADS * HEAD_DIM
    hrow = h_ref[0]
    proj = lambda lo, n: jnp.dot(hrow, wg_ref[:, lo:lo + n], preferred_element_type=F32)
    zg = proj(3 * D, 2 * W)
    gates = [proj(k * D, D) for k in range(3)]
    oa = _head_norm_gate(gf_ref[0] + gb_ref[0], zg[:, :W], gnw_ref[...])
    oc = _head_norm_gate(rf_ref[0] + rb_ref[0], zg[:, W:], rnw_ref[...])
    ya = jnp.dot(oa, wa_ref[...], preferred_element_type=F32)
    yb = jnp.dot(mo_ref[0], wb_ref[...], preferred_element_type=F32)
    yc = jnp.dot(oc, wc_ref[...], preferred_element_type=F32)
    mix = _sigmoid_tanh(gates[0]) * ya + _sigmoid_tanh(gates[1]) * yb + _sigmoid_tanh(gates[2]) * yc
    y = jnp.dot(mix.astype(BF16), wo_ref[...], preferred_element_type=F32)
    x1 = x_ref[0] + m[2:3] * y
    x1_ref[0] = x1
    h2_ref[0] = (_rms(x1, n2_ref[...]) * (1.0 + m[4:5]) + m[3:4]).astype(BF16)


def _merge(x9, mod9, h9, gdn_f, gdn_b, ret_f, ret_b, mla_o, gdn_nw, ret_nw, wg, wa, wb, wc, wo, n2, nslab):
    S9, N, D = x9.shape
    tm = min(256, N)
    W = HEADS * HEAD_DIM
    row = lambda w: pl.BlockSpec((1, tm, w), lambda b, i: (b, i, 0))
    full = lambda a: pl.BlockSpec(a.shape, lambda b, i: (0,) * a.ndim, pipeline_mode=pl.Buffered(1))
    gnw = jnp.tile(gdn_nw, HEADS).reshape(1, W)
    rnw = ret_nw.reshape(1, W)
    n2 = n2.reshape(1, D)
    return pl.pallas_call(
        _merge_kernel,
        grid=(nslab, N // tm),
        in_specs=[row(D), pl.BlockSpec((1, 6, D), lambda b, i: (b, 0, 0)), row(D),
                  row(W), row(W), row(W), row(W), row(W),
                  full(gnw), full(rnw), full(wg), full(wa), full(wb), full(wc), full(wo), full(n2)],
        out_specs=[row(D), row(D)],
        out_shape=[jax.ShapeDtypeStruct((nslab, N, D), F32), jax.ShapeDtypeStruct((nslab, N, D), BF16)],
        compiler_params=_cparams(("parallel", "parallel")),
        name="merge",
    )(x9, mod9, h9, gdn_f, gdn_b, ret_f, ret_b, mla_o, gnw, rnw, wg, wa, wb, wc, wo, n2)


FFN_CHUNK = 256
FFN_LOOKAHEAD = 2


def _ffn_kernel(h_ref, hp_ref, hn_ref, x_ref, mod_ref, wup_ref, cw_ref, cb_ref, wd_ref, nw_ref, nmod_ref,
                *out_refs, nlat, seq_lat, seq_ctx, final):
    b, i = pl.program_id(0), pl.program_id(1)
    tm = h_ref.shape[1]
    FF = wd_ref.shape[0]
    nch = FF // FFN_CHUNK
    seq_len, is_ctx = _slab_seq(b, nlat, seq_lat, seq_ctx)
    he = _rows_with_halo(h_ref, hp_ref, hn_ref, i * tm, seq_len)

    def up(c):
        g = slice(c * FFN_CHUNK, (c + 1) * FFN_CHUNK)
        v = slice(FF + c * FFN_CHUNK, FF + (c + 1) * FFN_CHUNK)
        return (jnp.dot(he, wup_ref[:, g], preferred_element_type=F32),
                jnp.dot(he, wup_ref[:, v], preferred_element_type=F32))

    def act(c, ug, uv):
        g = slice(c * FFN_CHUNK, (c + 1) * FFN_CHUNK)
        v = slice(FF + c * FFN_CHUNK, FF + (c + 1) * FFN_CHUNK)
        gate = _conv3_ext(ug, cw_ref[:, g], seq_ctx, is_ctx) + cb_ref[:, g]
        val = _conv3_ext(uv, cw_ref[:, v], seq_ctx, is_ctx) + cb_ref[:, v]
        return (_silu(gate) * val).astype(BF16)

    ups = [up(c) for c in range(min(FFN_LOOKAHEAD, nch))]
    acc = None
    for c in range(nch):
        if c + FFN_LOOKAHEAD < nch:
            ups.append(up(c + FFN_LOOKAHEAD))
        part = jnp.dot(act(c, *ups[c]), wd_ref[c * FFN_CHUNK:(c + 1) * FFN_CHUNK, :], preferred_element_type=F32)
        ups[c] = None
        acc = part if acc is None else acc + part

    m = mod_ref[0]
    x2 = x_ref[0] + m[5:6] * acc
    if final:
        out_refs[0][0] = _rms(x2, nw_ref[...])
    else:
        nm = nmod_ref[0]
        out_refs[0][0] = x2
        out_refs[1][0] = (_rms(x2, nw_ref[...]) * (1.0 + nm[1:2]) + nm[0:1]).astype(BF16)


def _ffn(h2, x1, mod9, w_up, conv_w, conv_b, w_down, next_nw, next_mod9, nslab, nlat, seq_ctx, final):
    S9, N, D = x1.shape
    FF = w_down.shape[0]
    tm = min(512, N)
    row = lambda: pl.BlockSpec((1, tm, D), lambda b, i: (b, i, 0))
    modspec = pl.BlockSpec((1, 6, D), lambda b, i: (b, 0, 0))
    resident = lambda a: pl.BlockSpec(a.shape, lambda b, i: (0,) * a.ndim, pipeline_mode=pl.Buffered(1))
    cb = conv_b.reshape(1, 2 * FF)
    nw = next_nw.reshape(1, D)
    if final:
        out_specs = [row()]
        out_shape = [jax.ShapeDtypeStruct((nslab, N, D), F32)]
    else:
        out_specs = [row(), row()]
        out_shape = [jax.ShapeDtypeStruct((S9, N, D), F32), jax.ShapeDtypeStruct((S9, N, D), BF16)]
    return pl.pallas_call(
        functools.partial(_ffn_kernel, nlat=nlat, seq_lat=N, seq_ctx=seq_ctx, final=final),
        grid=(nslab, N // tm),
        in_specs=_halo_specs(tm, N, D) + [
            row(), modspec, resident(w_up), resident(conv_w), resident(cb), resident(w_down),
            pl.BlockSpec((1, D), lambda b, i: (0, 0)), modspec],
        out_specs=out_specs,
        out_shape=out_shape,
        compiler_params=_cparams(("parallel", "parallel")),
        name="ffn_final" if final else "ffn",
    )(h2, h2, h2, x1, mod9, w_up, conv_w, cb, w_down, nw, next_mod9)


def _axial_angles(n, d):
    rows = n // GRID_W
    r = jnp.repeat(jnp.arange(rows, dtype=F32), GRID_W)
    col = jnp.tile(jnp.arange(GRID_W, dtype=F32), rows)
    quarter = d // 4
    inv = ROPE_BASE ** (-jnp.arange(quarter, dtype=F32) / quarter)
    return jnp.concatenate([r[:, None] * inv, col[:, None] * inv], axis=-1)


def _rope_tables(n):
    a_r = _axial_angles(n, HEAD_DIM)
    cr, sr = jnp.cos(a_r), jnp.sin(a_r)
    cos_ret = jnp.stack([jnp.concatenate([cr, cr], -1), jnp.ones((n, HEAD_DIM), F32)])
    sin_ret = jnp.stack([jnp.concatenate([-sr, sr], -1), jnp.zeros((n, HEAD_DIM), F32)])
    a_m = _axial_angles(n, MLA_ROPE)
    cm, sm = jnp.cos(a_m), jnp.sin(a_m)
    cos_k = jnp.concatenate([cm, cm], -1)
    sin_k = jnp.concatenate([sm, sm], -1)
    pad = MLA_HEAD_PAD - MLA_NOPE - MLA_ROPE
    cos_q = jnp.concatenate([jnp.ones((n, MLA_NOPE), F32), cos_k, jnp.zeros((n, pad), F32)], -1)
    sin_q = jnp.concatenate([jnp.zeros((n, MLA_NOPE), F32), sin_k, jnp.zeros((n, pad), F32)], -1)
    ident_q = jnp.concatenate([jnp.ones((n, MLA_NOPE + MLA_ROPE), F32), jnp.zeros((n, pad), F32)], -1)
    return (cos_ret, sin_ret,
            jnp.stack([cos_q, ident_q]), jnp.stack([sin_q, jnp.zeros_like(sin_q)]),
            jnp.stack([cos_k, jnp.ones_like(cos_k)]), jnp.stack([sin_k, jnp.zeros_like(sin_k)]))


def _rot_half_cols(w):
    half = w.shape[-1] // 2
    return jnp.concatenate([-w[..., half:], w[..., :half]], axis=-1)


def kernel(x, c, ctx, c_ctx, ada_w, ada_b, norm1_w, w_in, gdn_conv_w, gdn_A_log, gdn_dt_bias, gdn_norm_w, mla_q_norm, mla_w_uq, mla_kv_norm, mla_w_ukv, ret_norm_w, w_br_gdn, w_br_mla, w_br_ret, w_out, norm2_w, ffn_w_up, ffn_conv_w, ffn_conv_b, ffn_w_down, final_norm_w):
    B, N, D = x.shape
    M = ctx.shape[1]
    L = ada_w.shape[0]
    W = HEADS * HEAD_DIM
    assert B * M == N and N % SEQ_BLOCK == 0 and M % SEQ_BLOCK == 0
    assert N & (N - 1) == 0 and M & (M - 1) == 0 and D == 1024

    x9 = jnp.concatenate([x, ctx.reshape(1, N, D)], axis=0)
    cc = jnp.zeros((16, D), F32).at[:B].set(c).at[B].set(c_ctx)
    mod = _modulation(cc, ada_w.astype(BF16), ada_b)[:, :B + 1].reshape(L, B + 1, 6, D)
    cos_ret, sin_ret, cos_q, sin_q, cos_k, sin_k = _rope_tables(N)

    h = _norm_mod(x9, mod[0], norm1_w[0])
    out = None
    for l in range(L):
        last = l == L - 1
        nslab = B if last else B + 1
        wi = w_in[l]
        o_qkv, o_z, o_ab = 3 * W, 4 * W, 4 * W + 16
        o_cq = o_ab
        o_ckv = o_cq + MLA_Q_RANK
        o_kr = o_ckv + MLA_KV_RANK
        o_ret = o_kr + MLA_ROPE
        o_gate = o_ret + 4 * W
        w_kr = wi[:, o_kr:o_ret]
        w_ab = wi[:, o_z:o_ab]
        zc = lambda n: jnp.zeros((D, n), F32)
        w_mla = jnp.concatenate([wi[:, o_cq:o_kr], w_kr, zc(64), _rot_half_cols(w_kr), zc(64), w_ab, zc(112)],
                                axis=1).astype(BF16)
        assert w_mla.shape[1] == P_AB + 128
        w_gates = jnp.concatenate([wi[:, o_gate:], wi[:, o_qkv:o_z], wi[:, o_ret + 3 * W:o_gate]], axis=1).astype(BF16)

        qkv = _gdn_proj(h, wi[:, :o_qkv].astype(BF16), gdn_conv_w[l], B, M)
        ret_qkv = _ret_proj(h, wi[:, o_ret:o_ret + 3 * W].astype(BF16), cos_ret, sin_ret, B)

        wq = mla_w_uq[l].reshape(MLA_Q_RANK, HEADS, MLA_NOPE + MLA_ROPE)
        zq = jnp.zeros((MLA_Q_RANK, HEADS, MLA_HEAD_PAD - MLA_NOPE - MLA_ROPE), F32)
        wq_p = jnp.concatenate([wq, zq], -1).reshape(MLA_Q_RANK, HEADS * MLA_HEAD_PAD).astype(BF16)
        wq_r = jnp.concatenate([jnp.zeros_like(wq[..., :MLA_NOPE]), _rot_half_cols(wq[..., MLA_NOPE:]), zq],
                               -1).reshape(MLA_Q_RANK, HEADS * MLA_HEAD_PAD).astype(BF16)
        wkv = mla_w_ukv[l].reshape(MLA_KV_RANK, HEADS, 2 * HEAD_DIM)
        wk = wkv[..., :MLA_NOPE].reshape(MLA_KV_RANK, W).astype(BF16)
        wv = wkv[..., MLA_NOPE:].reshape(MLA_KV_RANK, W).astype(BF16)
        Q, K, V, ab, abT = _mla_prep(h, w_mla, w_ab.T.astype(BF16), mla_q_norm[l], wq_p,
                                     wq_r, mla_kv_norm[l], wk, wv, cos_q, sin_q, cos_k, sin_k, B)
        gdn_f, gdn_b = _gdn_scan(qkv, ab, abT, gdn_A_log[l], gdn_dt_bias[l], B, M)
        ret_f, ret_b = _ret_scan(ret_qkv, B, M)
        mla_o = _mla_attn(Q, K, V, B, M, with_ctx=not last)

        x1, h2 = _merge(x9, mod[l], h, gdn_f, gdn_b, ret_f, ret_b, mla_o, gdn_norm_w[l], ret_norm_w[l], w_gates,
                        w_br_gdn[l].astype(BF16), w_br_mla[l].astype(BF16), w_br_ret[l].astype(BF16),
                        w_out[l].astype(BF16), norm2_w[l], nslab)
        ffn_args = (h2, x1, mod[l], ffn_w_up[l].astype(BF16), ffn_conv_w[l], ffn_conv_b[l],
                    ffn_w_down[l].astype(BF16))
        if last:
            (out,) = _ffn(*ffn_args, final_norm_w, mod[l], nslab, B, M, True)
        else:
            x9, h = _ffn(*ffn_args, norm1_w[l + 1], mod[l + 1], nslab, B, M, False)
    return out
```

```python
import functools
import math

import jax
import jax.numpy as jnp
from jax import lax
from jax.experimental import pallas as pl
from jax.experimental.pallas import tpu as pltpu

F32 = jnp.float32
BF16 = jnp.bfloat16
HIGHEST = lax.Precision.HIGHEST

EPS = 1e-6
GRID_W = 64
ROPE_BASE = 10000.0
HEADS = 4
HEAD_DIM = 128
CHUNK = 64
MLA_Q_RANK = 384
MLA_KV_RANK = 256
MLA_NOPE = 128
MLA_ROPE = 64
MLA_SCALE = (MLA_NOPE + MLA_ROPE) ** -0.5
MLA_HEAD_PAD = 256
RET_DECAY_BASE = 5.0
RET_DIR_OFFSET = 0.5
QK_SCALE = HEAD_DIM ** -0.5
NEG_BIG = -1e30

P_MLA = 0
P_KR = 640
P_KR_ROT = 768
P_AB = 896
P_GATES = 1024
P_RET = 4096
P_Z = 6144
P_COLS = 6656

V7X_VMEM_LIMIT = 56 * 1024 * 1024
SEQ_BLOCK = 256
HALO = 16


def _cparams(sem):
    return pltpu.CompilerParams(dimension_semantics=sem, vmem_limit_bytes=V7X_VMEM_LIMIT)


def _bdot(a, b):
    return jnp.dot(a.astype(BF16), b.astype(BF16), preferred_element_type=F32)


def _bdot_nt(a, b):
    return lax.dot_general(a.astype(BF16), b.astype(BF16), (((1,), (1,)), ((), ())),
                           preferred_element_type=F32)


def _bdot_tn(a, b):
    return lax.dot_general(a.astype(BF16), b.astype(BF16), (((0,), (0,)), ((), ())),
                           preferred_element_type=F32)


def _rms(x, w):
    return x * lax.rsqrt(jnp.mean(x * x, axis=-1, keepdims=True) + EPS) * w


def _silu(x):
    return x * (1.0 / (1.0 + jnp.exp(-x)))


def _sigmoid(x):
    return 1.0 / (1.0 + jnp.exp(-x))


def _mod_kernel(c_ref, w_ref, b_ref, o_ref):
    a = _silu(c_ref[...])
    o_ref[0] = _bdot(a, w_ref[0]) + b_ref[0]


def _modulation(cc, ada_w, ada_b):
    L, D, D6 = ada_w.shape
    R = cc.shape[0]
    tn = 1536
    return pl.pallas_call(
        _mod_kernel,
        grid=(L, D6 // tn),
        in_specs=[pl.BlockSpec((R, D), lambda l, j: (0, 0)),
                  pl.BlockSpec((1, D, tn), lambda l, j: (l, 0, j)),
                  pl.BlockSpec((1, 1, tn), lambda l, j: (l, 0, j))],
        out_specs=pl.BlockSpec((1, R, tn), lambda l, j: (l, 0, j)),
        out_shape=jax.ShapeDtypeStruct((L, R, D6), F32),
        compiler_params=_cparams(("parallel", "parallel")),
        name="modulation",
    )(cc, ada_w, ada_b.reshape(L, 1, D6))


def _norm_mod_kernel(x_ref, mod_ref, nw_ref, h_ref):
    m = mod_ref[0]
    h = _rms(x_ref[0], nw_ref[...]) * (1.0 + m[1:2]) + m[0:1]
    h_ref[0] = h.astype(BF16)


def _norm_mod(x9, mod9, nw):
    S, N, D = x9.shape
    tm = min(512, N)
    return pl.pallas_call(
        _norm_mod_kernel,
        grid=(S, N // tm),
        in_specs=[pl.BlockSpec((1, tm, D), lambda b, i: (b, i, 0)),
                  pl.BlockSpec((1, 6, D), lambda b, i: (b, 0, 0)),
                  pl.BlockSpec((1, D), lambda b, i: (0, 0))],
        out_specs=pl.BlockSpec((1, tm, D), lambda b, i: (b, i, 0)),
        out_shape=jax.ShapeDtypeStruct((S, N, D), BF16),
        compiler_params=_cparams(("parallel", "parallel")),
        name="norm_mod",
    )(x9, mod9, nw.reshape(1, D))


def _rows_with_halo(h_ref, hp_ref, hn_ref, row0, seq_len):
    tm = h_ref.shape[1]
    hp, hn = hp_ref[0], hn_ref[0]
    hp = jnp.where((row0 & (seq_len - 1)) == 0, jnp.zeros_like(hp), hp)
    hn = jnp.where(((row0 + tm) & (seq_len - 1)) == 0, jnp.zeros_like(hn), hn)
    return jnp.concatenate([hp, h_ref[0], hn], axis=0)


def _conv3_ext(ue, cw, short_len, is_short):
    n = ue.shape[0]
    tm = n - 2 * HALO
    mid = slice(HALO, n - HALO)
    y = cw[0:1] * pltpu.roll(ue, 1, 0)[mid] + cw[1:2] * ue[mid] + cw[2:3] * pltpu.roll(ue, n - 1, 0)[mid]
    if short_len >= tm:
        return y
    rows = lax.broadcasted_iota(jnp.int32, (2 * 8, 1), 0)
    pieces, at = [], 0
    for edge in range(short_len, tm, short_len):
        blk = ue[HALO + edge - 8:HALO + edge + 8]
        fix = (jnp.where(rows == 7, is_short, 0.0) * (cw[2:3] * blk[8:9])
               + jnp.where(rows == 8, is_short, 0.0) * (cw[0:1] * blk[7:8]))
        pieces += [y[at:edge - 8], y[edge - 8:edge + 8] - fix]
        at = edge + 8
    return jnp.concatenate(pieces + [y[at:]], axis=0)


def _halo_specs(tm, N, D):
    nb = tm // HALO
    last = N // HALO - 1
    return [pl.BlockSpec((1, tm, D), lambda b, i, *_: (b, i, 0)),
            pl.BlockSpec((1, HALO, D), lambda b, i, *_: (b, jnp.maximum(i * nb - 1, 0), 0)),
            pl.BlockSpec((1, HALO, D), lambda b, i, *_: (b, jnp.minimum((i + 1) * nb, last), 0))]


def _slab_seq(b, nlat, seq_lat, seq_ctx):
    is_ctx = b == nlat
    return jnp.where(is_ctx, seq_ctx, seq_lat), jnp.where(is_ctx, 1.0, 0.0)


def _gdn_proj_kernel(h_ref, hp_ref, hn_ref, w_ref, cw_ref, o_ref, *, nlat, seq_lat, seq_ctx):
    b, i, j = pl.program_id(0), pl.program_id(1), pl.program_id(2)
    tm = h_ref.shape[1]
    seq_len, is_ctx = _slab_seq(b, nlat, seq_lat, seq_ctx)
    he = _rows_with_halo(h_ref, hp_ref, hn_ref, i * tm, seq_len)
    ue = jnp.dot(he, w_ref[...], preferred_element_type=F32)
    y = _silu(_conv3_ext(ue, cw_ref[...], seq_ctx, is_ctx))
    qk_scale = jnp.where(j == 0, QK_SCALE, 1.0)
    for h in range(HEADS):
        yh = y[:, h * HEAD_DIM:(h + 1) * HEAD_DIM]
        r = lax.rsqrt(jnp.sum(yh * yh, axis=-1, keepdims=True) + EPS)
        f = jnp.where(j == 2, 1.0, r * qk_scale)
        o_ref[0, :, h * HEAD_DIM:(h + 1) * HEAD_DIM] = yh * f


def _gdn_proj(h9, w_qkv, conv_w, nlat, seq_ctx):
    S, N, D = h9.shape
    C = w_qkv.shape[1]
    tm = min(1024, N)
    tn = HEADS * HEAD_DIM
    return pl.pallas_call(
        functools.partial(_gdn_proj_kernel, nlat=nlat, seq_lat=N, seq_ctx=seq_ctx),
        grid=(S, N // tm, C // tn),
        in_specs=_halo_specs(tm, N, D) + [
            pl.BlockSpec((D, tn), lambda b, i, j: (0, j)),
            pl.BlockSpec((3, tn), lambda b, i, j: (0, j))],
        out_specs=pl.BlockSpec((1, tm, tn), lambda b, i, j: (b, i, j)),
        out_shape=jax.ShapeDtypeStruct((S, N, C), F32),
        compiler_params=_cparams(("parallel", "parallel", "arbitrary")),
        name="gdn_proj",
    )(h9, h9, h9, w_qkv, conv_w)


def _seq_maps(nlat, n_ctx_blk, n_lat_blk):
    def fwd(b, s):
        is_ctx = s < n_ctx_blk
        return jnp.where(is_ctx, nlat, b), jnp.where(is_ctx, b * n_ctx_blk + s, s - n_ctx_blk)

    def bwd(b, s):
        is_ctx = s < n_ctx_blk
        return (jnp.where(is_ctx, nlat, b),
                jnp.where(is_ctx, b * n_ctx_blk + n_ctx_blk - 1 - s, n_lat_blk - 1 - (s - n_ctx_blk)))

    return fwd, bwd


def _tri_masks(n, backward):
    r = lax.broadcasted_iota(jnp.int32, (n, n), 0)
    c = lax.broadcasted_iota(jnp.int32, (n, n), 1)
    if backward:
        return r <= c, r < c
    return r >= c, r > c


def _gdn_gates(backward, ab_ref, abt_ref, alog_c, dtb_c, alog_r, dtb_r):
    ab = ab_ref[0]
    abt = abt_ref[0]
    T = ab.shape[0]

    def softplus(x):
        return jnp.maximum(x, 0.0) + jnp.log1p(jnp.exp(-jnp.abs(x)))

    g_c = -jnp.exp(alog_c) * softplus(ab + dtb_c)
    g_r = -jnp.exp(alog_r) * softplus(abt + dtb_r)
    beta_c = _sigmoid(ab)
    ri = lax.broadcasted_iota(jnp.int32, (T, T), 0)
    ci = lax.broadcasted_iota(jnp.int32, (T, T), 1)
    same = (ri // CHUNK) == (ci // CHUNK)
    cs = jnp.where(same & ((ci >= ri) if backward else (ci <= ri)), 1.0, 0.0).astype(BF16)

    def split3(g):
        hi = g.astype(BF16)
        r1 = g - hi.astype(F32)
        mid = r1.astype(BF16)
        lo = (r1 - mid.astype(F32)).astype(BF16)
        return hi, mid, lo

    pc = jnp.dot(cs, jnp.concatenate(split3(g_c), axis=1), preferred_element_type=F32)
    gcum_c = pc[:, :128] + pc[:, 128:256] + pc[:, 256:]
    pr = lax.dot_general(jnp.concatenate(split3(g_r), axis=0), cs, (((1,), (1,)), ((), ())),
                         preferred_element_type=F32)
    gcum_r = pr[:16] + pr[16:32] + pr[32:]
    return gcum_c, gcum_r, jnp.exp(gcum_c), beta_c


def _gdn_kernel(qf, kf, vf, abf, abtf, qb, kb, vb, abb, abtb, alog_c, dtb_c, alog_r, dtb_r,
                of_ref, ob_ref, sf_ref, sb_ref):
    @pl.when(pl.program_id(1) == 0)
    def _():
        sf_ref[...] = jnp.zeros_like(sf_ref)
        sb_ref[...] = jnp.zeros_like(sb_ref)

    ac, dc, ar, dr = alog_c[...], dtb_c[...], alog_r[...], dtb_r[...]
    C = CHUNK
    nchunk = qf.shape[1] // C
    eye = jnp.where(lax.broadcasted_iota(jnp.int32, (C, C), 0)
                    == lax.broadcasted_iota(jnp.int32, (C, C), 1), 1.0, 0.0)
    dirs = []
    for d, (q_ref, k_ref, v_ref, ab_ref, abt_ref, o_ref, s_ref) in enumerate(
            ((qf, kf, vf, abf, abtf, of_ref, sf_ref), (qb, kb, vb, abb, abtb, ob_ref, sb_ref))):
        backward = d == 1
        gates = _gdn_gates(backward, ab_ref, abt_ref, ac, dc, ar, dr)
        dirs.append(dict(d=d, backward=backward, q_ref=q_ref, k_ref=k_ref, v_ref=v_ref, o_ref=o_ref,
                         s_ref=s_ref, gates=gates, masks=_tri_masks(C, backward)))

    units = []
    for pos in range(nchunk):
        for dr_ in dirs:
            c = nchunk - 1 - pos if dr_["backward"] else pos
            for h in range(HEADS):
                units.append(dict(dr_, h=h, c=c, pos=pos))

    for u in units:
        gcum_c, gcum_r, eg_c, beta_c = u["gates"]
        incl, strict = u["masks"]
        d, h, c = u["d"], u["h"], u["c"]
        ia, ib = 4 * d + h, 8 + 4 * d + h
        rs = slice(c * C, (c + 1) * C)
        hs = slice(h * HEAD_DIM, (h + 1) * HEAD_DIM)
        u["rs"], u["hs"] = rs, hs
        q = u["q_ref"][0, rs, hs]
        k = u["k_ref"][0, rs, hs]
        v = u["v_ref"][0, rs, hs]
        gc = gcum_c[rs, ia:ia + 1]
        gr = gcum_r[ia:ia + 1, rs]
        egc = eg_c[rs, ia:ia + 1]
        beta = beta_c[rs, ib:ib + 1]
        last = c * C if u["backward"] else (c + 1) * C - 1
        gtot = gcum_c[last:last + 1, ia:ia + 1]
        decay = jnp.exp(jnp.where(incl, gc - gr, NEG_BIG))
        kbeta = k * beta
        qk = _bdot_nt(jnp.concatenate([q, kbeta], axis=0), k)
        u["attn"] = (qk[:C] * decay).astype(BF16)
        x = -(qk[C:] * jnp.where(strict, decay, 0.0))
        u["p"] = x
        u["t"] = eye + x
        u["rhs"] = jnp.concatenate([v * beta, kbeta * egc], axis=1).astype(BF16)
        u["qd"] = (q * egc).astype(BF16)
        u["kd"] = (k * jnp.exp(gtot - gc)).astype(BF16)
        u["cdec"] = jnp.exp(gtot)

    for u in units:
        u["p"] = _bdot(u["p"], u["p"])
    for _ in range(int(math.log2(C)) - 2):
        for u in units:
            r = _bdot(jnp.concatenate([u["p"], u["t"]], axis=0), u["p"])
            u["t"] = u["t"] + r[C:]
            u["p"] = r[:C]
    for u in units:
        u["t"] = u["t"] + _bdot(u["t"], u["p"])

    for u in units:
        u["sol"] = _bdot(u["t"], u["rhs"])

    S = {(dr_["d"], h): dr_["s_ref"][h] for dr_ in dirs for h in range(HEADS)}
    for pos in range(nchunk):
        cur = [u for u in units if u["pos"] == pos]
        for u in cur:
            lhs = jnp.concatenate([u["sol"][:, HEAD_DIM:].astype(BF16), u["qd"]], axis=0)
            u["ws"] = _bdot(lhs, S[u["d"], u["h"]])
        for u in cur:
            key = (u["d"], u["h"])
            v_new = (u["sol"][:, :HEAD_DIM] - u["ws"][:C]).astype(BF16)
            u["o_ref"][0, u["rs"], u["hs"]] = u["ws"][C:] + _bdot(u["attn"], v_new)
            S[key] = u["cdec"] * S[key] + _bdot_tn(u["kd"], v_new)
    for dr_ in dirs:
        for h in range(HEADS):
            dr_["s_ref"][h] = S[dr_["d"], h]


def _gdn_scan(qkv, ab, abT, A_log, dt_bias, nlat, seq_ctx):
    S9, N, _ = qkv.shape
    T = SEQ_BLOCK
    ncb, nlb = seq_ctx // T, N // T
    fwd, bwd = _seq_maps(nlat, ncb, nlb)
    W = HEADS * HEAD_DIM

    def specs(m):
        return [pl.BlockSpec((1, T, W), lambda b, s, m=m: (*m(b, s), 0)),
                pl.BlockSpec((1, T, W), lambda b, s, m=m: (*m(b, s), 1)),
                pl.BlockSpec((1, T, W), lambda b, s, m=m: (*m(b, s), 2)),
                pl.BlockSpec((1, T, 128), lambda b, s, m=m: (*m(b, s), 0)),
                pl.BlockSpec((1, 16, T), lambda b, s, m=m: (m(b, s)[0], 0, m(b, s)[1]))]

    small = lambda shape: pl.BlockSpec(shape, lambda b, s: (0, 0))
    alog = A_log.reshape(-1)
    dtb = dt_bias.reshape(-1)
    pad_c = lambda t: jnp.zeros((1, 128), F32).at[0, :8].set(t)
    pad_r = lambda t: jnp.zeros((16, 1), F32).at[:8, 0].set(t)
    out_sds = jax.ShapeDtypeStruct((S9, N, W), F32)
    return pl.pallas_call(
        _gdn_kernel,
        grid=(nlat, ncb + nlb),
        in_specs=specs(fwd) + specs(bwd) + [small((1, 128)), small((1, 128)), small((16, 1)), small((16, 1))],
        out_specs=[pl.BlockSpec((1, T, W), lambda b, s: (*fwd(b, s), 0)),
                   pl.BlockSpec((1, T, W), lambda b, s: (*bwd(b, s), 0))],
        out_shape=[out_sds, out_sds],
        scratch_shapes=[pltpu.VMEM((HEADS, HEAD_DIM, HEAD_DIM), F32),
                        pltpu.VMEM((HEADS, HEAD_DIM, HEAD_DIM), F32)],
        compiler_params=_cparams(("parallel", "arbitrary")),
        name="gdn_scan",
    )(qkv, qkv, qkv, ab, abT, qkv, qkv, qkv, ab, abT, pad_c(alog), pad_c(dtb), pad_r(alog), pad_r(dtb))


def _ret_log_gamma(d, h):
    return math.log1p(-(2.0 ** (-(RET_DECAY_BASE + h + RET_DIR_OFFSET * d))))


def _ret_kernel(h_ref, w_ref, cos_ref, sin_ref, o_ref, sf_ref, sb_ref, sb_all_ref, k_keep_ref, v_keep_ref,
                *, n_ctx_blk, n_lat_blk):
    sweep, s = pl.program_id(1), pl.program_id(2)
    W = HEADS * HEAD_DIM
    C = h_ref.shape[1]
    half = HEAD_DIM // 2
    hrow = h_ref[0]
    cos, sin = cos_ref[0], sin_ref[0]
    pos = lax.broadcasted_iota(jnp.int32, (C, 1), 0).astype(F32)
    proj = lambda j: jnp.dot(hrow, w_ref[:, j * W:(j + 1) * W], preferred_element_type=F32)
    rope = lambda t: t * cos + pltpu.roll(t, half, 1) * sin
    head_slices = [slice(h * HEAD_DIM, (h + 1) * HEAD_DIM) for h in range(HEADS)]

    @pl.when(sweep == 0)
    def _():
        @pl.when(s == 0)
        def _():
            sb_ref[...] = jnp.zeros_like(sb_ref)

        k_all, v_all = proj(1), proj(2).astype(BF16)
        v_keep_ref[s] = v_all
        for h, hs in enumerate(head_slices):
            lg = _ret_log_gamma(1, h)
            k = rope(k_all[:, hs])
            k_keep_ref[s, :, hs] = k
            S = sb_ref[h]
            sb_all_ref[s * HEADS + h] = S
            sb_ref[h] = math.exp(lg * C) * S + _bdot_tn(k * jnp.exp(lg * pos), v_all[:, hs])

    @pl.when(sweep == 1)
    def _():
        @pl.when(s == 0)
        def _():
            sf_ref[...] = jnp.zeros_like(sf_ref)

        seen = jnp.where(s < n_ctx_blk, n_ctx_blk - 1 - s, 2 * n_ctx_blk + n_lat_blk - 1 - s)
        q_all = proj(0)
        r = lax.broadcasted_iota(jnp.int32, (C, C), 0)
        c = lax.broadcasted_iota(jnp.int32, (C, C), 1)
        dist = jnp.abs(r - c).astype(F32)
        units = []
        for h, hs in enumerate(head_slices):
            k = k_keep_ref[seen, :, hs]
            v = v_keep_ref[seen, :, hs]
            lf, lb = _ret_log_gamma(0, h), _ret_log_gamma(1, h)
            q = rope(q_all[:, hs]) * QK_SCALE
            decay = jnp.where(r >= c, jnp.exp(lf * dist), 0.0) + jnp.where(r <= c, jnp.exp(lb * dist), 0.0)
            Sf = sf_ref[h]
            q2 = jnp.concatenate([q * jnp.exp(lf * (pos + 1.0)), q * jnp.exp(lb * (C - pos))], axis=1)
            S2 = jnp.concatenate([Sf, sb_all_ref[seen * HEADS + h]], axis=0)
            units.append((h, hs, v, decay, _bdot_nt(q, k), _bdot(q2, S2),
                          math.exp(lf * C) * Sf + _bdot_tn(k * jnp.exp(lf * (C - 1 - pos)), v)))
        for h, hs, v, decay, qk, inter, s_new in units:
            o_ref[0, :, hs] = _bdot((qk * decay).astype(BF16), v) + inter
            sf_ref[h] = s_new


def _ret_scan(h9, w_ret, cos2, sin2, nlat, seq_ctx):
    S9, N, D = h9.shape
    T = SEQ_BLOCK
    ncb, nlb = seq_ctx // T, N // T
    fwd, bwd = _seq_maps(nlat, ncb, nlb)
    W = HEADS * HEAD_DIM

    def blk(b, sweep, s):
        f, w = fwd(b, s), bwd(b, s)
        return jnp.where(sweep == 0, w[0], f[0]), jnp.where(sweep == 0, w[1], f[1])

    def tab(b, sweep, s):
        slab, i = blk(b, sweep, s)
        return jnp.where(slab == nlat, 1, 0), jnp.where(slab == nlat, 0, i), 0

    return pl.pallas_call(
        functools.partial(_ret_kernel, n_ctx_blk=ncb, n_lat_blk=nlb),
        grid=(nlat, 2, ncb + nlb),
        in_specs=[pl.BlockSpec((1, T, D), lambda b, sweep, s: (*blk(b, sweep, s), 0)),
                  pl.BlockSpec(w_ret.shape, lambda b, sweep, s: (0, 0), pipeline_mode=pl.Buffered(1)),
                  pl.BlockSpec((1, T, HEAD_DIM), tab), pl.BlockSpec((1, T, HEAD_DIM), tab)],
        out_specs=pl.BlockSpec((1, T, W), lambda b, sweep, s: (*fwd(b, jnp.where(sweep == 0, 0, s)), 0)),
        out_shape=jax.ShapeDtypeStruct((S9, N, W), F32),
        scratch_shapes=[pltpu.VMEM((HEADS, HEAD_DIM, HEAD_DIM), F32),
                        pltpu.VMEM((HEADS, HEAD_DIM, HEAD_DIM), F32),
                        pltpu.VMEM(((ncb + nlb) * HEADS, HEAD_DIM, HEAD_DIM), F32),
                        pltpu.VMEM((ncb + nlb, T, W), F32),
                        pltpu.VMEM((ncb + nlb, T, W), BF16)],
        compiler_params=_cparams(("parallel", "arbitrary", "arbitrary")),
        name="ret_scan",
    )(h9, w_ret, cos2, sin2)


def _mla_prep_kernel(h_ref, wm_ref, wabt_ref, qn_ref, wq_ref, wqr_ref, kvn_ref, wk_ref, wv_ref,
                     cq_ref, sq_ref, ck_ref, sk_ref, q_ref, k_ref, v_ref, ab_ref, abt_ref):
    hrow = h_ref[0]
    p = jnp.dot(hrow, wm_ref[...], preferred_element_type=F32)
    ab_ref[0] = p[:, P_AB:P_AB + 128]
    abt_ref[0] = lax.dot_general(wabt_ref[...], hrow, (((1,), (1,)), ((), ())), preferred_element_type=F32)
    cq = _rms(p[:, :MLA_Q_RANK], qn_ref[...]).astype(BF16)
    ckv = _rms(p[:, MLA_Q_RANK:MLA_Q_RANK + MLA_KV_RANK], kvn_ref[...]).astype(BF16)
    cosq, sinq = cq_ref[0], sq_ref[0]
    q = jnp.dot(cq, wq_ref[...], preferred_element_type=F32)
    qr = jnp.dot(cq, wqr_ref[...], preferred_element_type=F32)
    for h in range(HEADS):
        hs = slice(h * MLA_HEAD_PAD, (h + 1) * MLA_HEAD_PAD)
        q_ref[0, :, hs] = (q[:, hs] * cosq + qr[:, hs] * sinq).astype(BF16)
    kn = jnp.dot(ckv, wk_ref[...], preferred_element_type=F32)
    v_ref[0] = jnp.dot(ckv, wv_ref[...], preferred_element_type=F32).astype(BF16)
    krope = (p[:, P_KR:P_KR + MLA_ROPE] * ck_ref[0] + p[:, P_KR_ROT:P_KR_ROT + MLA_ROPE] * sk_ref[0]).astype(BF16)
    zpad = jnp.zeros((p.shape[0], MLA_HEAD_PAD - MLA_NOPE - MLA_ROPE), BF16)
    for h in range(HEADS):
        o = h * MLA_HEAD_PAD
        k_ref[0, :, o:o + MLA_NOPE] = kn[:, h * MLA_NOPE:(h + 1) * MLA_NOPE].astype(BF16)
        k_ref[0, :, o + MLA_NOPE:o + MLA_NOPE + MLA_ROPE] = krope
        k_ref[0, :, o + MLA_NOPE + MLA_ROPE:o + MLA_HEAD_PAD] = zpad


def _mla_prep(h9, w_mla, w_abt, q_norm, wq, wq_rot, kv_norm, wk, wv, cosq, sinq, cosk, sink, nlat):
    S9, N, D = h9.shape
    tm = min(512, N)
    QW = HEADS * MLA_HEAD_PAD
    VW = HEADS * HEAD_DIM
    full = lambda a: pl.BlockSpec(a.shape, lambda b, i: (0,) * a.ndim)
    tab = lambda w: pl.BlockSpec((1, tm, w), lambda b, i: (jnp.where(b == nlat, 1, 0), jnp.where(b == nlat, 0, i), 0))
    row = lambda w: pl.BlockSpec((1, tm, w), lambda b, i: (b, i, 0))
    qn = q_norm.reshape(1, -1)
    kvn = kv_norm.reshape(1, -1)
    return pl.pallas_call(
        _mla_prep_kernel,
        grid=(S9, N // tm),
        in_specs=[row(D), full(w_mla), full(w_abt),
                  full(qn), full(wq), full(wq_rot), full(kvn), full(wk), full(wv),
                  tab(MLA_HEAD_PAD), tab(MLA_HEAD_PAD), tab(MLA_ROPE), tab(MLA_ROPE)],
        out_specs=[row(QW), row(QW), row(VW), row(128),
                   pl.BlockSpec((1, 16, tm), lambda b, i: (b, 0, i))],
        out_shape=[jax.ShapeDtypeStruct((S9, N, QW), BF16),
                   jax.ShapeDtypeStruct((S9, N, QW), BF16),
                   jax.ShapeDtypeStruct((S9, N, VW), BF16),
                   jax.ShapeDtypeStruct((S9, N, 128), F32),
                   jax.ShapeDtypeStruct((S9, 16, N), F32)],
        compiler_params=_cparams(("parallel", "parallel")),
        name="mla_prep",
    )(h9, w_mla, w_abt, qn, wq, wq_rot, kvn, wk, wv, cosq, sinq, cosk, sink)


MLA_SCALE_LOG2E = MLA_SCALE * math.log2(math.e)


def _mla_attn_kernel(q_ref, kl_ref, kc_ref, vl_ref, vc_ref, o_ref, *, n_lat_tiles):
    nt = lambda a, b: lax.dot_general(a, b, (((1,), (1,)), ((), ())), preferred_element_type=F32)

    def heads(latent_keys):
        def scores(h):
            hs = slice(h * MLA_HEAD_PAD, (h + 1) * MLA_HEAD_PAD)
            q = q_ref[0, :, hs]
            s2 = nt(q, kc_ref[0, :, hs])
            return (nt(q, kl_ref[0, :, hs]), s2) if latent_keys else (s2,)

        def probs(*ss):
            m = functools.reduce(jnp.maximum, [jnp.max(s, axis=-1, keepdims=True) for s in ss])
            mc = m * MLA_SCALE_LOG2E
            ps = [jnp.exp2(s * MLA_SCALE_LOG2E - mc) for s in ss]
            l = functools.reduce(jnp.add, [jnp.sum(p, axis=-1, keepdims=True) for p in ps])
            return [p.astype(BF16) for p in ps], l

        def out(h, ps, l):
            hs = slice(h * HEAD_DIM, (h + 1) * HEAD_DIM)
            o = jnp.dot(ps[-1], vc_ref[0, :, hs], preferred_element_type=F32)
            if latent_keys:
                o = o + jnp.dot(ps[0], vl_ref[0, :, hs], preferred_element_type=F32)
            o_ref[0, :, hs] = (o / l).astype(BF16)

        s, p = {}, {}
        for step in range(HEADS + 2):
            if step < HEADS:
                s[step] = scores(step)
            if 0 <= step - 1 < HEADS:
                p[step - 1] = probs(*s.pop(step - 1))
            if 0 <= step - 2 < HEADS:
                out(step - 2, *p.pop(step - 2))

    i = pl.program_id(1)
    pl.when(i < n_lat_tiles)(lambda: heads(True))
    pl.when(i >= n_lat_tiles)(lambda: heads(False))


def _mla_attn(Q, K, V, nlat, seq_ctx, with_ctx):
    S9, N, QW = Q.shape
    VW = V.shape[2]
    M = seq_ctx
    tq = M
    nq = N // tq

    def qmap(b, i):
        is_ctx = i >= nq
        return jnp.where(is_ctx, nlat, b), jnp.where(is_ctx, b, i), 0

    return pl.pallas_call(
        functools.partial(_mla_attn_kernel, n_lat_tiles=nq),
        grid=(nlat, nq + 1 if with_ctx else nq),
        in_specs=[pl.BlockSpec((1, tq, QW), qmap),
                  pl.BlockSpec((1, N, QW), lambda b, i: (b, 0, 0)),
                  pl.BlockSpec((1, M, QW), lambda b, i: (nlat, b, 0)),
                  pl.BlockSpec((1, N, VW), lambda b, i: (b, 0, 0)),
                  pl.BlockSpec((1, M, VW), lambda b, i: (nlat, b, 0))],
        out_specs=pl.BlockSpec((1, tq, VW), qmap),
        out_shape=jax.ShapeDtypeStruct((S9 if with_ctx else nlat, N, VW), BF16),
        compiler_params=_cparams(("parallel", "arbitrary")),
        name="mla_attn",
    )(Q, K, K, V, V)


def _head_norm_gate(o, gate, w):
    parts = []
    for h in range(HEADS):
        hs = slice(h * HEAD_DIM, (h + 1) * HEAD_DIM)
        parts.append(_rms(o[:, hs], w[:, hs]))
    return (jnp.concatenate(parts, axis=1) * _silu(gate)).astype(BF16)


def _sigmoid_tanh(x):
    return 0.5 * jnp.tanh(0.5 * x) + 0.5


def _merge_kernel(x_ref, mod_ref, h_ref, gf_ref, gb_ref, ro_ref, mo_ref,
                  gnw_ref, rnw_ref, wg_ref, wa_ref, wb_ref, wc_ref, wo_ref, n2_ref, x1_ref, h2_ref):
    m = mod_ref[0]
    D = x_ref.shape[2]
    W = HEADS * HEAD_DIM
    hrow = h_ref[0]
    proj = lambda lo, n: jnp.dot(hrow, wg_ref[:, lo:lo + n], preferred_element_type=F32)
    zg = proj(3 * D, 2 * W)
    gates = [proj(k * D, D) for k in range(3)]
    oa = _head_norm_gate(gf_ref[0] + gb_ref[0], zg[:, :W], gnw_ref[...])
    oc = _head_norm_gate(ro_ref[0], zg[:, W:], rnw_ref[...])
    ya = jnp.dot(oa, wa_ref[...], preferred_element_type=F32)
    yb = jnp.dot(mo_ref[0], wb_ref[...], preferred_element_type=F32)
    yc = jnp.dot(oc, wc_ref[...], preferred_element_type=F32)
    mix = _sigmoid_tanh(gates[0]) * ya + _sigmoid_tanh(gates[1]) * yb + _sigmoid_tanh(gates[2]) * yc
    y = jnp.dot(mix.astype(BF16), wo_ref[...], preferred_element_type=F32)
    x1 = x_ref[0] + m[2:3] * y
    x1_ref[0] = x1
    h2_ref[0] = (_rms(x1, n2_ref[...]) * (1.0 + m[4:5]) + m[3:4]).astype(BF16)


def _merge(x9, mod9, h9, gdn_f, gdn_b, ret_o, mla_o, gdn_nw, ret_nw, wg, wa, wb, wc, wo, n2, nslab):
    S9, N, D = x9.shape
    tm = min(256, N)
    W = HEADS * HEAD_DIM
    row = lambda w: pl.BlockSpec((1, tm, w), lambda b, i: (b, i, 0))
    full = lambda a: pl.BlockSpec(a.shape, lambda b, i: (0,) * a.ndim, pipeline_mode=pl.Buffered(1))
    gnw = jnp.tile(gdn_nw, HEADS).reshape(1, W)
    rnw = ret_nw.reshape(1, W)
    n2 = n2.reshape(1, D)
    return pl.pallas_call(
        _merge_kernel,
        grid=(nslab, N // tm),
        in_specs=[row(D), pl.BlockSpec((1, 6, D), lambda b, i: (b, 0, 0)), row(D),
                  row(W), row(W), row(W), row(W),
                  full(gnw), full(rnw), full(wg), full(wa), full(wb), full(wc), full(wo), full(n2)],
        out_specs=[row(D), row(D)],
        out_shape=[jax.ShapeDtypeStruct((nslab, N, D), F32), jax.ShapeDtypeStruct((nslab, N, D), BF16)],
        compiler_params=_cparams(("parallel", "parallel")),
        name="merge",
    )(x9, mod9, h9, gdn_f, gdn_b, ret_o, mla_o, gnw, rnw, wg, wa, wb, wc, wo, n2)


FFN_CHUNK = 256
FFN_LOOKAHEAD = 2


def _ffn_kernel(h_ref, hp_ref, hn_ref, x_ref, mod_ref, wup_ref, cw_ref, cb_ref, wd_ref, nw_ref, nmod_ref,
                *out_refs, nlat, seq_lat, seq_ctx, final):
    b, i = pl.program_id(0), pl.program_id(1)
    tm = h_ref.shape[1]
    FF = wd_ref.shape[0]
    nch = FF // FFN_CHUNK
    seq_len, is_ctx = _slab_seq(b, nlat, seq_lat, seq_ctx)
    he = _rows_with_halo(h_ref, hp_ref, hn_ref, i * tm, seq_len)

    def up(c):
        g = slice(c * FFN_CHUNK, (c + 1) * FFN_CHUNK)
        v = slice(FF + c * FFN_CHUNK, FF + (c + 1) * FFN_CHUNK)
        return (jnp.dot(he, wup_ref[:, g], preferred_element_type=F32),
                jnp.dot(he, wup_ref[:, v], preferred_element_type=F32))

    def act(c, ug, uv):
        g = slice(c * FFN_CHUNK, (c + 1) * FFN_CHUNK)
        v = slice(FF + c * FFN_CHUNK, FF + (c + 1) * FFN_CHUNK)
        gate = _conv3_ext(ug, cw_ref[:, g], seq_ctx, is_ctx) + cb_ref[:, g]
        val = _conv3_ext(uv, cw_ref[:, v], seq_ctx, is_ctx) + cb_ref[:, v]
        return (_silu(gate) * val).astype(BF16)

    ups = [up(c) for c in range(min(FFN_LOOKAHEAD, nch))]
    acc = None
    for c in range(nch):
        if c + FFN_LOOKAHEAD < nch:
            ups.append(up(c + FFN_LOOKAHEAD))
        part = jnp.dot(act(c, *ups[c]), wd_ref[c * FFN_CHUNK:(c + 1) * FFN_CHUNK, :], preferred_element_type=F32)
        ups[c] = None
        acc = part if acc is None else acc + part

    m = mod_ref[0]
    x2 = x_ref[0] + m[5:6] * acc
    if final:
        out_refs[0][0] = _rms(x2, nw_ref[...])
    else:
        nm = nmod_ref[0]
        out_refs[0][0] = x2
        out_refs[1][0] = (_rms(x2, nw_ref[...]) * (1.0 + nm[1:2]) + nm[0:1]).astype(BF16)


def _ffn(h2, x1, mod9, w_up, conv_w, conv_b, w_down, next_nw, next_mod9, nslab, nlat, seq_ctx, final):
    S9, N, D = x1.shape
    FF = w_down.shape[0]
    tm = min(512, N)
    row = lambda: pl.BlockSpec((1, tm, D), lambda b, i: (b, i, 0))
    modspec = pl.BlockSpec((1, 6, D), lambda b, i: (b, 0, 0))
    resident = lambda a: pl.BlockSpec(a.shape, lambda b, i: (0,) * a.ndim, pipeline_mode=pl.Buffered(1))
    cb = conv_b.reshape(1, 2 * FF)
    nw = next_nw.reshape(1, D)
    if final:
        out_specs = [row()]
        out_shape = [jax.ShapeDtypeStruct((nslab, N, D), F32)]
    else:
        out_specs = [row(), row()]
        out_shape = [jax.ShapeDtypeStruct((S9, N, D), F32), jax.ShapeDtypeStruct((S9, N, D), BF16)]
    return pl.pallas_call(
        functools.partial(_ffn_kernel, nlat=nlat, seq_lat=N, seq_ctx=seq_ctx, final=final),
        grid=(nslab, N // tm),
        in_specs=_halo_specs(tm, N, D) + [
            row(), modspec, resident(w_up), resident(conv_w), resident(cb), resident(w_down),
            pl.BlockSpec((1, D), lambda b, i: (0, 0)), modspec],
        out_specs=out_specs,
        out_shape=out_shape,
        compiler_params=_cparams(("parallel", "parallel")),
        name="ffn_final" if final else "ffn",
    )(h2, h2, h2, x1, mod9, w_up, conv_w, cb, w_down, nw, next_mod9)


def _axial_angles(n, d):
    rows = n // GRID_W
    r = jnp.repeat(jnp.arange(rows, dtype=F32), GRID_W)
    col = jnp.tile(jnp.arange(GRID_W, dtype=F32), rows)
    quarter = d // 4
    inv = ROPE_BASE ** (-jnp.arange(quarter, dtype=F32) / quarter)
    return jnp.concatenate([r[:, None] * inv, col[:, None] * inv], axis=-1)


def _rope_tables(n):
    a_r = _axial_angles(n, HEAD_DIM)
    cr, sr = jnp.cos(a_r), jnp.sin(a_r)
    cos_ret = jnp.stack([jnp.concatenate([cr, cr], -1), jnp.ones((n, HEAD_DIM), F32)])
    sin_ret = jnp.stack([jnp.concatenate([-sr, sr], -1), jnp.zeros((n, HEAD_DIM), F32)])
    a_m = _axial_angles(n, MLA_ROPE)
    cm, sm = jnp.cos(a_m), jnp.sin(a_m)
    cos_k = jnp.concatenate([cm, cm], -1)
    sin_k = jnp.concatenate([sm, sm], -1)
    pad = MLA_HEAD_PAD - MLA_NOPE - MLA_ROPE
    cos_q = jnp.concatenate([jnp.ones((n, MLA_NOPE), F32), cos_k, jnp.zeros((n, pad), F32)], -1)
    sin_q = jnp.concatenate([jnp.zeros((n, MLA_NOPE), F32), sin_k, jnp.zeros((n, pad), F32)], -1)
    ident_q = jnp.concatenate([jnp.ones((n, MLA_NOPE + MLA_ROPE), F32), jnp.zeros((n, pad), F32)], -1)
    return (cos_ret, sin_ret,
            jnp.stack([cos_q, ident_q]), jnp.stack([sin_q, jnp.zeros_like(sin_q)]),
            jnp.stack([cos_k, jnp.ones_like(cos_k)]), jnp.stack([sin_k, jnp.zeros_like(sin_k)]))


def _rot_half_cols(w):
    half = w.shape[-1] // 2
    return jnp.concatenate([-w[..., half:], w[..., :half]], axis=-1)


def kernel(x, c, ctx, c_ctx, ada_w, ada_b, norm1_w, w_in, gdn_conv_w, gdn_A_log, gdn_dt_bias, gdn_norm_w, mla_q_norm, mla_w_uq, mla_kv_norm, mla_w_ukv, ret_norm_w, w_br_gdn, w_br_mla, w_br_ret, w_out, norm2_w, ffn_w_up, ffn_conv_w, ffn_conv_b, ffn_w_down, final_norm_w):
    B, N, D = x.shape
    M = ctx.shape[1]
    L = ada_w.shape[0]
    W = HEADS * HEAD_DIM
    assert B * M == N and N % SEQ_BLOCK == 0 and M % SEQ_BLOCK == 0
    assert N & (N - 1) == 0 and M & (M - 1) == 0 and D == 1024

    x9 = jnp.concatenate([x, ctx.reshape(1, N, D)], axis=0)
    cc = jnp.zeros((16, D), F32).at[:B].set(c).at[B].set(c_ctx)
    mod = _modulation(cc, ada_w, ada_b)[:, :B + 1].reshape(L, B + 1, 6, D)
    cos_ret, sin_ret, cos_q, sin_q, cos_k, sin_k = _rope_tables(N)

    h = _norm_mod(x9, mod[0], norm1_w[0])
    out = None
    for l in range(L):
        last = l == L - 1
        nslab = B if last else B + 1
        wi = w_in[l]
        o_qkv, o_z, o_ab = 3 * W, 4 * W, 4 * W + 16
        o_cq = o_ab
        o_ckv = o_cq + MLA_Q_RANK
        o_kr = o_ckv + MLA_KV_RANK
        o_ret = o_kr + MLA_ROPE
        o_gate = o_ret + 4 * W
        w_kr = wi[:, o_kr:o_ret]
        w_ab = wi[:, o_z:o_ab]
        zc = lambda n: jnp.zeros((D, n), F32)
        w_mla = jnp.concatenate([wi[:, o_cq:o_kr], w_kr, zc(64), _rot_half_cols(w_kr), zc(64), w_ab, zc(112)],
                                axis=1).astype(BF16)
        assert w_mla.shape[1] == P_AB + 128
        w_gates = jnp.concatenate([wi[:, o_gate:], wi[:, o_qkv:o_z], wi[:, o_ret + 3 * W:o_gate]], axis=1).astype(BF16)

        qkv = _gdn_proj(h, wi[:, :o_qkv].astype(BF16), gdn_conv_w[l], B, M)
        ret_o = _ret_scan(h, wi[:, o_ret:o_ret + 3 * W].astype(BF16), cos_ret, sin_ret, B, M)

        wq = mla_w_uq[l].reshape(MLA_Q_RANK, HEADS, MLA_NOPE + MLA_ROPE)
        zq = jnp.zeros((MLA_Q_RANK, HEADS, MLA_HEAD_PAD - MLA_NOPE - MLA_ROPE), F32)
        wq_p = jnp.concatenate([wq, zq], -1).reshape(MLA_Q_RANK, HEADS * MLA_HEAD_PAD).astype(BF16)
        wq_r = jnp.concatenate([jnp.zeros_like(wq[..., :MLA_NOPE]), _rot_half_cols(wq[..., MLA_NOPE:]), zq],
                               -1).reshape(MLA_Q_RANK, HEADS * MLA_HEAD_PAD).astype(BF16)
        wkv = mla_w_ukv[l].reshape(MLA_KV_RANK, HEADS, 2 * HEAD_DIM)
        wk = wkv[..., :MLA_NOPE].reshape(MLA_KV_RANK, W).astype(BF16)
        wv = wkv[..., MLA_NOPE:].reshape(MLA_KV_RANK, W).astype(BF16)
        Q, K, V, ab, abT = _mla_prep(h, w_mla, w_ab.T.astype(BF16), mla_q_norm[l], wq_p,
                                     wq_r, mla_kv_norm[l], wk, wv, cos_q, sin_q, cos_k, sin_k, B)
        gdn_f, gdn_b = _gdn_scan(qkv, ab, abT, gdn_A_log[l], gdn_dt_bias[l], B, M)
        mla_o = _mla_attn(Q, K, V, B, M, with_ctx=not last)

        x1, h2 = _merge(x9, mod[l], h, gdn_f, gdn_b, ret_o, mla_o, gdn_norm_w[l], ret_norm_w[l], w_gates,
                        w_br_gdn[l].astype(BF16), w_br_mla[l].astype(BF16), w_br_ret[l].astype(BF16),
                        w_out[l].astype(BF16), norm2_w[l], nslab)
        ffn_args = (h2, x1, mod[l], ffn_w_up[l].astype(BF16), ffn_conv_w[l], ffn_conv_b[l],
                    ffn_w_down[l].astype(BF16))
        if last:
            (out,) = _ffn(*ffn_args, final_norm_w, mod[l], nslab, B, M, True)
        else:
            x9, h = _ffn(*ffn_args, norm1_w[l + 1], mod[l + 1], nslab, B, M, False)
    return out
```

```python
import functools
import math

import jax
import jax.numpy as jnp
from jax import lax
from jax.experimental import pallas as pl
from jax.experimental.pallas import tpu as pltpu

F32 = jnp.float32
BF16 = jnp.bfloat16

EPS = 1e-6
GRID_W = 64
ROPE_BASE = 10000.0
HEADS = 4
HEAD_DIM = 128
CHUNK = 64
MLA_Q_RANK = 384
MLA_KV_RANK = 256
MLA_NOPE = 128
MLA_ROPE = 64
MLA_SCALE = (MLA_NOPE + MLA_ROPE) ** -0.5
MLA_HEAD_PAD = 256
RET_DECAY_BASE = 5.0
RET_DIR_OFFSET = 0.5
QK_SCALE = HEAD_DIM ** -0.5
NEG_BIG = -1e30

P_KR = MLA_Q_RANK + MLA_KV_RANK
P_KR_ROT = P_KR + 128
P_AB = P_KR_ROT + 128

V7X_VMEM_LIMIT = 56 * 1024 * 1024
SEQ_BLOCK = 256
HALO = 16


def _cparams(sem):
    return pltpu.CompilerParams(dimension_semantics=sem, vmem_limit_bytes=V7X_VMEM_LIMIT)


def _bdot(a, b):
    return jnp.dot(a.astype(BF16), b.astype(BF16), preferred_element_type=F32)


def _bdot_nt(a, b):
    return lax.dot_general(a.astype(BF16), b.astype(BF16), (((1,), (1,)), ((), ())),
                           preferred_element_type=F32)


def _bdot_tn(a, b):
    return lax.dot_general(a.astype(BF16), b.astype(BF16), (((0,), (0,)), ((), ())),
                           preferred_element_type=F32)


def _rms(x, w):
    return x * lax.rsqrt(jnp.mean(x * x, axis=-1, keepdims=True) + EPS) * w


def _silu(x):
    return x * (1.0 / (1.0 + jnp.exp(-x)))


def _sigmoid(x):
    return 1.0 / (1.0 + jnp.exp(-x))


def _mod_kernel(c_ref, w_ref, b_ref, o_ref):
    a = _silu(c_ref[...])
    o_ref[0] = _bdot(a, w_ref[0]) + b_ref[0]


def _modulation(cc, ada_w, ada_b):
    L, D, D6 = ada_w.shape
    R = cc.shape[0]
    tn = 1536
    return pl.pallas_call(
        _mod_kernel,
        grid=(L, D6 // tn),
        in_specs=[pl.BlockSpec((R, D), lambda l, j: (0, 0)),
                  pl.BlockSpec((1, D, tn), lambda l, j: (l, 0, j)),
                  pl.BlockSpec((1, 1, tn), lambda l, j: (l, 0, j))],
        out_specs=pl.BlockSpec((1, R, tn), lambda l, j: (l, 0, j)),
        out_shape=jax.ShapeDtypeStruct((L, R, D6), F32),
        compiler_params=_cparams(("parallel", "parallel")),
        name="modulation",
    )(cc, ada_w, ada_b.reshape(L, 1, D6))


def _norm_mod_kernel(x_ref, mod_ref, nw_ref, h_ref):
    m = mod_ref[0]
    h = _rms(x_ref[0], nw_ref[...]) * (1.0 + m[1:2]) + m[0:1]
    h_ref[0] = h.astype(BF16)


def _norm_mod(x9, mod9, nw):
    S, N, D = x9.shape
    tm = min(512, N)
    return pl.pallas_call(
        _norm_mod_kernel,
        grid=(S, N // tm),
        in_specs=[pl.BlockSpec((1, tm, D), lambda b, i: (b, i, 0)),
                  pl.BlockSpec((1, 6, D), lambda b, i: (b, 0, 0)),
                  pl.BlockSpec((1, D), lambda b, i: (0, 0))],
        out_specs=pl.BlockSpec((1, tm, D), lambda b, i: (b, i, 0)),
        out_shape=jax.ShapeDtypeStruct((S, N, D), BF16),
        compiler_params=_cparams(("parallel", "parallel")),
        name="norm_mod",
    )(x9, mod9, nw.reshape(1, D))


def _rows_with_halo(h_ref, hp_ref, hn_ref, row0, seq_len):
    tm = h_ref.shape[1]
    hp, hn = hp_ref[0], hn_ref[0]
    hp = jnp.where((row0 & (seq_len - 1)) == 0, jnp.zeros_like(hp), hp)
    hn = jnp.where(((row0 + tm) & (seq_len - 1)) == 0, jnp.zeros_like(hn), hn)
    return jnp.concatenate([hp, h_ref[0], hn], axis=0)


def _conv3_ext(ue, cw, short_len, is_short):
    n = ue.shape[0]
    tm = n - 2 * HALO
    mid = slice(HALO, n - HALO)
    y = cw[0:1] * pltpu.roll(ue, 1, 0)[mid] + cw[1:2] * ue[mid] + cw[2:3] * pltpu.roll(ue, n - 1, 0)[mid]
    if short_len >= tm:
        return y
    rows = lax.broadcasted_iota(jnp.int32, (2 * 8, 1), 0)
    pieces, at = [], 0
    for edge in range(short_len, tm, short_len):
        blk = ue[HALO + edge - 8:HALO + edge + 8]
        fix = (jnp.where(rows == 7, is_short, 0.0) * (cw[2:3] * blk[8:9])
               + jnp.where(rows == 8, is_short, 0.0) * (cw[0:1] * blk[7:8]))
        pieces += [y[at:edge - 8], y[edge - 8:edge + 8] - fix]
        at = edge + 8
    return jnp.concatenate(pieces + [y[at:]], axis=0)


def _halo_specs(tm, N, D):
    nb = tm // HALO
    last = N // HALO - 1
    return [pl.BlockSpec((1, tm, D), lambda b, i, *_: (b, i, 0)),
            pl.BlockSpec((1, HALO, D), lambda b, i, *_: (b, jnp.maximum(i * nb - 1, 0), 0)),
            pl.BlockSpec((1, HALO, D), lambda b, i, *_: (b, jnp.minimum((i + 1) * nb, last), 0))]


def _slab_seq(b, nlat, seq_lat, seq_ctx):
    is_ctx = b == nlat
    return jnp.where(is_ctx, seq_ctx, seq_lat), jnp.where(is_ctx, 1.0, 0.0)


GDN_PROJ_CHUNK = 2 * HEAD_DIM
GDN_PROJ_LOOKAHEAD = 2


def _gdn_proj_kernel(h_ref, hp_ref, hn_ref, w_ref, cw_ref, o_ref, *, nlat, seq_lat, seq_ctx):
    b, i = pl.program_id(0), pl.program_id(1)
    tm = h_ref.shape[1]
    W = HEADS * HEAD_DIM
    nch = w_ref.shape[1] // GDN_PROJ_CHUNK
    seq_len, is_ctx = _slab_seq(b, nlat, seq_lat, seq_ctx)
    he = _rows_with_halo(h_ref, hp_ref, hn_ref, i * tm, seq_len)
    cols = lambda c: slice(c * GDN_PROJ_CHUNK, (c + 1) * GDN_PROJ_CHUNK)
    up = lambda c: jnp.dot(he, w_ref[:, cols(c)], preferred_element_type=F32)

    def finish(c, ue):
        y = _silu(_conv3_ext(ue, cw_ref[:, cols(c)], seq_ctx, is_ctx))
        part = c * GDN_PROJ_CHUNK // W
        for h in range(GDN_PROJ_CHUNK // HEAD_DIM):
            yh = y[:, h * HEAD_DIM:(h + 1) * HEAD_DIM]
            if part < 2:
                r = lax.rsqrt(jnp.sum(yh * yh, axis=-1, keepdims=True) + EPS)
                yh = yh * r * QK_SCALE if part == 0 else yh * r
            lo = c * GDN_PROJ_CHUNK + h * HEAD_DIM
            o_ref[0, :, lo:lo + HEAD_DIM] = yh

    ups = [up(c) for c in range(min(GDN_PROJ_LOOKAHEAD, nch))]
    for c in range(nch):
        if c + GDN_PROJ_LOOKAHEAD < nch:
            ups.append(up(c + GDN_PROJ_LOOKAHEAD))
        finish(c, ups[c])
        ups[c] = None


def _gdn_proj(h9, w_qkv, conv_w, nlat, seq_ctx):
    S, N, D = h9.shape
    C = w_qkv.shape[1]
    tm = min(1024, N)
    resident = lambda a: pl.BlockSpec(a.shape, lambda b, i: (0,) * a.ndim, pipeline_mode=pl.Buffered(1))
    return pl.pallas_call(
        functools.partial(_gdn_proj_kernel, nlat=nlat, seq_lat=N, seq_ctx=seq_ctx),
        grid=(S, N // tm),
        in_specs=_halo_specs(tm, N, D) + [resident(w_qkv), resident(conv_w)],
        out_specs=pl.BlockSpec((1, tm, C), lambda b, i: (b, i, 0)),
        out_shape=jax.ShapeDtypeStruct((S, N, C), F32),
        compiler_params=_cparams(("parallel", "parallel")),
        name="gdn_proj",
    )(h9, h9, h9, w_qkv, conv_w)


def _seq_maps(nlat, n_ctx_blk, n_lat_blk):
    def fwd(b, s):
        is_ctx = s < n_ctx_blk
        return jnp.where(is_ctx, nlat, b), jnp.where(is_ctx, b * n_ctx_blk + s, s - n_ctx_blk)

    def bwd(b, s):
        is_ctx = s < n_ctx_blk
        return (jnp.where(is_ctx, nlat, b),
                jnp.where(is_ctx, b * n_ctx_blk + n_ctx_blk - 1 - s, n_lat_blk - 1 - (s - n_ctx_blk)))

    return fwd, bwd


def _tri_masks(n, backward):
    r = lax.broadcasted_iota(jnp.int32, (n, n), 0)
    c = lax.broadcasted_iota(jnp.int32, (n, n), 1)
    if backward:
        return r <= c, r < c
    return r >= c, r > c


def _gdn_gates(backward, ab_ref, abt_ref, alog_c, dtb_c, alog_r, dtb_r):
    ab = ab_ref[0]
    abt = abt_ref[0]
    T = ab.shape[0]

    def softplus(x):
        return jnp.maximum(x, 0.0) + jnp.log1p(jnp.exp(-jnp.abs(x)))

    g_c = -jnp.exp(alog_c) * softplus(ab + dtb_c)
    g_r = -jnp.exp(alog_r) * softplus(abt + dtb_r)
    beta_c = _sigmoid(ab)
    ri = lax.broadcasted_iota(jnp.int32, (T, T), 0)
    ci = lax.broadcasted_iota(jnp.int32, (T, T), 1)
    same = (ri // CHUNK) == (ci // CHUNK)
    cs = jnp.where(same & ((ci >= ri) if backward else (ci <= ri)), 1.0, 0.0).astype(BF16)

    def split3(g):
        hi = g.astype(BF16)
        r1 = g - hi.astype(F32)
        mid = r1.astype(BF16)
        lo = (r1 - mid.astype(F32)).astype(BF16)
        return hi, mid, lo

    pc = jnp.dot(cs, jnp.concatenate(split3(g_c), axis=1), preferred_element_type=F32)
    gcum_c = pc[:, :128] + pc[:, 128:256] + pc[:, 256:]
    pr = lax.dot_general(jnp.concatenate(split3(g_r), axis=0), cs, (((1,), (1,)), ((), ())),
                         preferred_element_type=F32)
    gcum_r = pr[:16] + pr[16:32] + pr[32:]
    return gcum_c, gcum_r, jnp.exp(gcum_c), beta_c


def _gdn_kernel(qf, kf, vf, abf, abtf, qb, kb, vb, abb, abtb, alog_c, dtb_c, alog_r, dtb_r,
                of_ref, ob_ref, sf_ref, sb_ref):
    @pl.when(pl.program_id(1) == 0)
    def _():
        sf_ref[...] = jnp.zeros_like(sf_ref)
        sb_ref[...] = jnp.zeros_like(sb_ref)

    ac, dc, ar, dr = alog_c[...], dtb_c[...], alog_r[...], dtb_r[...]
    C = CHUNK
    nchunk = qf.shape[1] // C
    eye = jnp.where(lax.broadcasted_iota(jnp.int32, (C, C), 0)
                    == lax.broadcasted_iota(jnp.int32, (C, C), 1), 1.0, 0.0)
    dirs = []
    for d, (q_ref, k_ref, v_ref, ab_ref, abt_ref, o_ref, s_ref) in enumerate(
            ((qf, kf, vf, abf, abtf, of_ref, sf_ref), (qb, kb, vb, abb, abtb, ob_ref, sb_ref))):
        backward = d == 1
        gates = _gdn_gates(backward, ab_ref, abt_ref, ac, dc, ar, dr)
        dirs.append(dict(d=d, backward=backward, q_ref=q_ref, k_ref=k_ref, v_ref=v_ref, o_ref=o_ref,
                         s_ref=s_ref, gates=gates, masks=_tri_masks(C, backward)))

    units = []
    for pos in range(nchunk):
        for dr_ in dirs:
            c = nchunk - 1 - pos if dr_["backward"] else pos
            for h in range(HEADS):
                units.append(dict(dr_, h=h, c=c, pos=pos))

    for u in units:
        gcum_c, gcum_r, eg_c, beta_c = u["gates"]
        incl, strict = u["masks"]
        d, h, c = u["d"], u["h"], u["c"]
        ia, ib = 4 * d + h, 8 + 4 * d + h
        rs = slice(c * C, (c + 1) * C)
        hs = slice(h * HEAD_DIM, (h + 1) * HEAD_DIM)
        u["rs"], u["hs"] = rs, hs
        q = u["q_ref"][0, rs, hs]
        k = u["k_ref"][0, rs, hs]
        v = u["v_ref"][0, rs, hs]
        gc = gcum_c[rs, ia:ia + 1]
        gr = gcum_r[ia:ia + 1, rs]
        egc = eg_c[rs, ia:ia + 1]
        beta = beta_c[rs, ib:ib + 1]
        last = c * C if u["backward"] else (c + 1) * C - 1
        gtot = gcum_c[last:last + 1, ia:ia + 1]
        decay = jnp.exp(jnp.where(incl, gc - gr, NEG_BIG))
        kbeta = k * beta
        qk = _bdot_nt(jnp.concatenate([q, kbeta], axis=0), k)
        u["attn"] = (qk[:C] * decay).astype(BF16)
        x = -(qk[C:] * jnp.where(strict, decay, 0.0))
        u["p"] = x
        u["t"] = eye + x
        u["rhs"] = jnp.concatenate([v * beta, kbeta * egc], axis=1).astype(BF16)
        u["qd"] = (q * egc).astype(BF16)
        u["kd"] = (k * jnp.exp(gtot - gc)).astype(BF16)
        u["cdec"] = jnp.exp(gtot)

    for u in units:
        u["p"] = _bdot(u["p"], u["p"])
    for _ in range(int(math.log2(C)) - 2):
        for u in units:
            r = _bdot(jnp.concatenate([u["p"], u["t"]], axis=0), u["p"])
            u["t"] = u["t"] + r[C:]
            u["p"] = r[:C]
    for u in units:
        u["t"] = u["t"] + _bdot(u["t"], u["p"])

    for u in units:
        u["sol"] = _bdot(u["t"], u["rhs"])

    S = {(dr_["d"], h): dr_["s_ref"][h] for dr_ in dirs for h in range(HEADS)}
    for pos in range(nchunk):
        cur = [u for u in units if u["pos"] == pos]
        for u in cur:
            lhs = jnp.concatenate([u["sol"][:, HEAD_DIM:].astype(BF16), u["qd"]], axis=0)
            u["ws"] = _bdot(lhs, S[u["d"], u["h"]])
        for u in cur:
            key = (u["d"], u["h"])
            v_new = (u["sol"][:, :HEAD_DIM] - u["ws"][:C]).astype(BF16)
            u["o_ref"][0, u["rs"], u["hs"]] = u["ws"][C:] + _bdot(u["attn"], v_new)
            S[key] = u["cdec"] * S[key] + _bdot_tn(u["kd"], v_new)
    for dr_ in dirs:
        for h in range(HEADS):
            dr_["s_ref"][h] = S[dr_["d"], h]


def _gdn_scan(qkv, ab, abT, A_log, dt_bias, nlat, seq_ctx):
    S9, N, _ = qkv.shape
    T = SEQ_BLOCK
    ncb, nlb = seq_ctx // T, N // T
    fwd, bwd = _seq_maps(nlat, ncb, nlb)
    W = HEADS * HEAD_DIM

    def specs(m):
        return [pl.BlockSpec((1, T, W), lambda b, s, m=m: (*m(b, s), 0)),
                pl.BlockSpec((1, T, W), lambda b, s, m=m: (*m(b, s), 1)),
                pl.BlockSpec((1, T, W), lambda b, s, m=m: (*m(b, s), 2)),
                pl.BlockSpec((1, T, 128), lambda b, s, m=m: (*m(b, s), 0)),
                pl.BlockSpec((1, 16, T), lambda b, s, m=m: (m(b, s)[0], 0, m(b, s)[1]))]

    small = lambda shape: pl.BlockSpec(shape, lambda b, s: (0, 0))
    alog = A_log.reshape(-1)
    dtb = dt_bias.reshape(-1)
    pad_c = lambda t: jnp.zeros((1, 128), F32).at[0, :8].set(t)
    pad_r = lambda t: jnp.zeros((16, 1), F32).at[:8, 0].set(t)
    out_sds = jax.ShapeDtypeStruct((S9, N, W), F32)
    return pl.pallas_call(
        _gdn_kernel,
        grid=(nlat, ncb + nlb),
        in_specs=specs(fwd) + specs(bwd) + [small((1, 128)), small((1, 128)), small((16, 1)), small((16, 1))],
        out_specs=[pl.BlockSpec((1, T, W), lambda b, s: (*fwd(b, s), 0)),
                   pl.BlockSpec((1, T, W), lambda b, s: (*bwd(b, s), 0))],
        out_shape=[out_sds, out_sds],
        scratch_shapes=[pltpu.VMEM((HEADS, HEAD_DIM, HEAD_DIM), F32),
                        pltpu.VMEM((HEADS, HEAD_DIM, HEAD_DIM), F32)],
        compiler_params=_cparams(("parallel", "arbitrary")),
        name="gdn_scan",
    )(qkv, qkv, qkv, ab, abT, qkv, qkv, qkv, ab, abT, pad_c(alog), pad_c(dtb), pad_r(alog), pad_r(dtb))


def _ret_log_gamma(d, h):
    return math.log1p(-(2.0 ** (-(RET_DECAY_BASE + h + RET_DIR_OFFSET * d))))


def _ret_kernel(h_ref, w_ref, cos_ref, sin_ref, o_ref, sf_ref, sb_ref, sb_all_ref, k_keep_ref, v_keep_ref,
                decay_ref, *, n_ctx_blk, n_lat_blk):
    sweep, s = pl.program_id(1), pl.program_id(2)
    W = HEADS * HEAD_DIM
    C = h_ref.shape[1]
    half = HEAD_DIM // 2
    hrow = h_ref[0]
    cos, sin = cos_ref[0], sin_ref[0]
    pos = lax.broadcasted_iota(jnp.int32, (C, 1), 0).astype(F32)
    proj = lambda j: jnp.dot(hrow, w_ref[:, j * W:(j + 1) * W], preferred_element_type=F32)
    rope = lambda t: t * cos + pltpu.roll(t, half, 1) * sin
    head_slices = [slice(h * HEAD_DIM, (h + 1) * HEAD_DIM) for h in range(HEADS)]

    @pl.when(sweep == 0)
    def _():
        @pl.when(s == 0)
        def _():
            sb_ref[...] = jnp.zeros_like(sb_ref)

        k_all, v_all = proj(1), proj(2).astype(BF16)
        v_keep_ref[s] = v_all
        for h, hs in enumerate(head_slices):
            lg = _ret_log_gamma(1, h)
            k = rope(k_all[:, hs])
            k_keep_ref[s, :, hs] = k
            S = sb_ref[h]
            sb_all_ref[s * HEADS + h] = S
            sb_ref[h] = math.exp(lg * C) * S + _bdot_tn(k * jnp.exp(lg * pos), v_all[:, hs])

    @pl.when(sweep == 1)
    def _():
        @pl.when(s == 0)
        def _():
            sf_ref[...] = jnp.zeros_like(sf_ref)
            r = lax.broadcasted_iota(jnp.int32, (C, C), 0)
            c = lax.broadcasted_iota(jnp.int32, (C, C), 1)
            dist = jnp.abs(r - c).astype(F32)
            for h in range(HEADS):
                lf, lb = _ret_log_gamma(0, h), _ret_log_gamma(1, h)
                decay_ref[h] = (jnp.where(r >= c, jnp.exp(lf * dist), 0.0)
                                + jnp.where(r <= c, jnp.exp(lb * dist), 0.0))

        seen = jnp.where(s < n_ctx_blk, n_ctx_blk - 1 - s, 2 * n_ctx_blk + n_lat_blk - 1 - s)
        q_all = proj(0)
        units = []
        for h, hs in enumerate(head_slices):
            k = k_keep_ref[seen, :, hs]
            v = v_keep_ref[seen, :, hs]
            lf, lb = _ret_log_gamma(0, h), _ret_log_gamma(1, h)
            q = rope(q_all[:, hs]) * QK_SCALE
            Sf = sf_ref[h]
            q2 = jnp.concatenate([q * jnp.exp(lf * (pos + 1.0)), q * jnp.exp(lb * (C - pos))], axis=1)
            S2 = jnp.concatenate([Sf, sb_all_ref[seen * HEADS + h]], axis=0)
            units.append((h, hs, v, _bdot_nt(q, k), _bdot(q2, S2),
                          math.exp(lf * C) * Sf + _bdot_tn(k * jnp.exp(lf * (C - 1 - pos)), v)))
        for h, hs, v, qk, inter, s_new in units:
            o_ref[0, :, hs] = _bdot((qk * decay_ref[h]).astype(BF16), v) + inter
            sf_ref[h] = s_new


def _ret_scan(h9, w_ret, cos2, sin2, nlat, seq_ctx):
    S9, N, D = h9.shape
    T = SEQ_BLOCK
    ncb, nlb = seq_ctx // T, N // T
    fwd, bwd = _seq_maps(nlat, ncb, nlb)
    W = HEADS * HEAD_DIM

    def blk(b, sweep, s):
        f, w = fwd(b, s), bwd(b, s)
        return jnp.where(sweep == 0, w[0], f[0]), jnp.where(sweep == 0, w[1], f[1])

    def tab(b, sweep, s):
        slab, i = blk(b, sweep, s)
        return jnp.where(slab == nlat, 1, 0), jnp.where(slab == nlat, 0, i), 0

    return pl.pallas_call(
        functools.partial(_ret_kernel, n_ctx_blk=ncb, n_lat_blk=nlb),
        grid=(nlat, 2, ncb + nlb),
        in_specs=[pl.BlockSpec((1, T, D), lambda b, sweep, s: (*blk(b, sweep, s), 0)),
                  pl.BlockSpec(w_ret.shape, lambda b, sweep, s: (0, 0), pipeline_mode=pl.Buffered(1)),
                  pl.BlockSpec((1, T, HEAD_DIM), tab), pl.BlockSpec((1, T, HEAD_DIM), tab)],
        out_specs=pl.BlockSpec((1, T, W), lambda b, sweep, s: (*fwd(b, jnp.where(sweep == 0, 0, s)), 0)),
        out_shape=jax.ShapeDtypeStruct((S9, N, W), F32),
        scratch_shapes=[pltpu.VMEM((HEADS, HEAD_DIM, HEAD_DIM), F32),
                        pltpu.VMEM((HEADS, HEAD_DIM, HEAD_DIM), F32),
                        pltpu.VMEM(((ncb + nlb) * HEADS, HEAD_DIM, HEAD_DIM), F32),
                        pltpu.VMEM((ncb + nlb, T, W), F32),
                        pltpu.VMEM((ncb + nlb, T, W), BF16),
                        pltpu.VMEM((HEADS, T, T), F32)],
        compiler_params=_cparams(("parallel", "arbitrary", "arbitrary")),
        name="ret_scan",
    )(h9, w_ret, cos2, sin2)


def _mla_prep_kernel(h_ref, wm_ref, wabt_ref, qn_ref, wq_ref, wqr_ref, kvn_ref, wk_ref, wv_ref,
                     cq_ref, sq_ref, ck_ref, sk_ref, q_ref, k_ref, v_ref, ab_ref, abt_ref):
    hrow = h_ref[0]
    p = jnp.dot(hrow, wm_ref[...], preferred_element_type=F32)
    ab_ref[0] = p[:, P_AB:P_AB + 128]
    abt_ref[0] = lax.dot_general(wabt_ref[...], hrow, (((1,), (1,)), ((), ())), preferred_element_type=F32)
    cq = _rms(p[:, :MLA_Q_RANK], qn_ref[...]).astype(BF16)
    ckv = _rms(p[:, MLA_Q_RANK:MLA_Q_RANK + MLA_KV_RANK], kvn_ref[...]).astype(BF16)
    cosq, sinq = cq_ref[0], sq_ref[0]
    q = jnp.dot(cq, wq_ref[...], preferred_element_type=F32)
    qr = jnp.dot(cq, wqr_ref[...], preferred_element_type=F32)
    for h in range(HEADS):
        hs = slice(h * MLA_HEAD_PAD, (h + 1) * MLA_HEAD_PAD)
        q_ref[0, :, hs] = (q[:, hs] * cosq + qr[:, hs] * sinq).astype(BF16)
    kn = jnp.dot(ckv, wk_ref[...], preferred_element_type=F32)
    v_ref[0] = jnp.dot(ckv, wv_ref[...], preferred_element_type=F32).astype(BF16)
    krope = (p[:, P_KR:P_KR + MLA_ROPE] * ck_ref[0] + p[:, P_KR_ROT:P_KR_ROT + MLA_ROPE] * sk_ref[0]).astype(BF16)
    zpad = jnp.zeros((p.shape[0], MLA_HEAD_PAD - MLA_NOPE - MLA_ROPE), BF16)
    for h in range(HEADS):
        o = h * MLA_HEAD_PAD
        k_ref[0, :, o:o + MLA_NOPE] = kn[:, h * MLA_NOPE:(h + 1) * MLA_NOPE].astype(BF16)
        k_ref[0, :, o + MLA_NOPE:o + MLA_NOPE + MLA_ROPE] = krope
        k_ref[0, :, o + MLA_NOPE + MLA_ROPE:o + MLA_HEAD_PAD] = zpad


def _mla_prep(h9, w_mla, w_abt, q_norm, wq, wq_rot, kv_norm, wk, wv, cosq, sinq, cosk, sink, nlat):
    S9, N, D = h9.shape
    tm = min(512, N)
    QW = HEADS * MLA_HEAD_PAD
    VW = HEADS * HEAD_DIM
    full = lambda a: pl.BlockSpec(a.shape, lambda b, i: (0,) * a.ndim)
    tab = lambda w: pl.BlockSpec((1, tm, w), lambda b, i: (jnp.where(b == nlat, 1, 0), jnp.where(b == nlat, 0, i), 0))
    row = lambda w: pl.BlockSpec((1, tm, w), lambda b, i: (b, i, 0))
    qn = q_norm.reshape(1, -1)
    kvn = kv_norm.reshape(1, -1)
    return pl.pallas_call(
        _mla_prep_kernel,
        grid=(S9, N // tm),
        in_specs=[row(D), full(w_mla), full(w_abt),
                  full(qn), full(wq), full(wq_rot), full(kvn), full(wk), full(wv),
                  tab(MLA_HEAD_PAD), tab(MLA_HEAD_PAD), tab(MLA_ROPE), tab(MLA_ROPE)],
        out_specs=[row(QW), row(QW), row(VW), row(128),
                   pl.BlockSpec((1, 16, tm), lambda b, i: (b, 0, i))],
        out_shape=[jax.ShapeDtypeStruct((S9, N, QW), BF16),
                   jax.ShapeDtypeStruct((S9, N, QW), BF16),
                   jax.ShapeDtypeStruct((S9, N, VW), BF16),
                   jax.ShapeDtypeStruct((S9, N, 128), F32),
                   jax.ShapeDtypeStruct((S9, 16, N), F32)],
        compiler_params=_cparams(("parallel", "parallel")),
        name="mla_prep",
    )(h9, w_mla, w_abt, qn, wq, wq_rot, kvn, wk, wv, cosq, sinq, cosk, sink)


MLA_SCALE_LOG2E = MLA_SCALE * math.log2(math.e)


def _mla_attn_kernel(q_ref, kl_ref, kc_ref, vl_ref, vc_ref, o_ref, *, n_lat_tiles):
    nt = lambda a, b: lax.dot_general(a, b, (((1,), (1,)), ((), ())), preferred_element_type=F32)

    def heads(latent_keys):
        def scores(h):
            hs = slice(h * MLA_HEAD_PAD, (h + 1) * MLA_HEAD_PAD)
            q = q_ref[0, :, hs]
            s2 = nt(q, kc_ref[0, :, hs])
            return (nt(q, kl_ref[0, :, hs]), s2) if latent_keys else (s2,)

        def probs(*ss):
            m = functools.reduce(jnp.maximum, [jnp.max(s, axis=-1, keepdims=True) for s in ss])
            mc = m * MLA_SCALE_LOG2E
            ps = [jnp.exp2(s * MLA_SCALE_LOG2E - mc) for s in ss]
            l = functools.reduce(jnp.add, [jnp.sum(p, axis=-1, keepdims=True) for p in ps])
            return [p.astype(BF16) for p in ps], l

        def out(h, ps, l):
            hs = slice(h * HEAD_DIM, (h + 1) * HEAD_DIM)
            o = jnp.dot(ps[-1], vc_ref[0, :, hs], preferred_element_type=F32)
            if latent_keys:
                o = o + jnp.dot(ps[0], vl_ref[0, :, hs], preferred_element_type=F32)
            o_ref[0, :, hs] = (o / l).astype(BF16)

        s, p = {}, {}
        for step in range(HEADS + 2):
            if step < HEADS:
                s[step] = scores(step)
            if 0 <= step - 1 < HEADS:
                p[step - 1] = probs(*s.pop(step - 1))
            if 0 <= step - 2 < HEADS:
                out(step - 2, *p.pop(step - 2))

    i = pl.program_id(1)
    pl.when(i < n_lat_tiles)(lambda: heads(True))
    pl.when(i >= n_lat_tiles)(lambda: heads(False))


def _mla_attn(Q, K, V, nlat, seq_ctx, with_ctx):
    S9, N, QW = Q.shape
    VW = V.shape[2]
    M = seq_ctx
    tq = M
    nq = N // tq

    def qmap(b, i):
        is_ctx = i >= nq
        return jnp.where(is_ctx, nlat, b), jnp.where(is_ctx, b, i), 0

    return pl.pallas_call(
        functools.partial(_mla_attn_kernel, n_lat_tiles=nq),
        grid=(nlat, nq + 1 if with_ctx else nq),
        in_specs=[pl.BlockSpec((1, tq, QW), qmap),
                  pl.BlockSpec((1, N, QW), lambda b, i: (b, 0, 0)),
                  pl.BlockSpec((1, M, QW), lambda b, i: (nlat, b, 0)),
                  pl.BlockSpec((1, N, VW), lambda b, i: (b, 0, 0)),
                  pl.BlockSpec((1, M, VW), lambda b, i: (nlat, b, 0))],
        out_specs=pl.BlockSpec((1, tq, VW), qmap),
        out_shape=jax.ShapeDtypeStruct((S9 if with_ctx else nlat, N, VW), BF16),
        compiler_params=_cparams(("parallel", "arbitrary")),
        name="mla_attn",
    )(Q, K, K, V, V)


def _head_norm_gate(o, gate, w):
    parts = []
    for h in range(HEADS):
        hs = slice(h * HEAD_DIM, (h + 1) * HEAD_DIM)
        parts.append(_rms(o[:, hs], w[:, hs]))
    return (jnp.concatenate(parts, axis=1) * _silu(gate)).astype(BF16)


def _sigmoid_tanh(x):
    return 0.5 * jnp.tanh(0.5 * x) + 0.5


def _merge_kernel(x_ref, mod_ref, h_ref, gf_ref, gb_ref, ro_ref, mo_ref,
                  gnw_ref, rnw_ref, wg_ref, wa_ref, wb_ref, wc_ref, wo_ref, n2_ref, x1_ref, h2_ref):
    m = mod_ref[0]
    D = x_ref.shape[2]
    W = HEADS * HEAD_DIM
    hrow = h_ref[0]
    proj = lambda lo, n: jnp.dot(hrow, wg_ref[:, lo:lo + n], preferred_element_type=F32)
    zg = proj(3 * D, 2 * W)
    gates = [proj(k * D, D) for k in range(3)]
    oa = _head_norm_gate(gf_ref[0] + gb_ref[0], zg[:, :W], gnw_ref[...])
    oc = _head_norm_gate(ro_ref[0], zg[:, W:], rnw_ref[...])
    ya = jnp.dot(oa, wa_ref[...], preferred_element_type=F32)
    yb = jnp.dot(mo_ref[0], wb_ref[...], preferred_element_type=F32)
    yc = jnp.dot(oc, wc_ref[...], preferred_element_type=F32)
    mix = _sigmoid_tanh(gates[0]) * ya + _sigmoid_tanh(gates[1]) * yb + _sigmoid_tanh(gates[2]) * yc
    y = jnp.dot(mix.astype(BF16), wo_ref[...], preferred_element_type=F32)
    x1 = x_ref[0] + m[2:3] * y
    x1_ref[0] = x1
    h2_ref[0] = (_rms(x1, n2_ref[...]) * (1.0 + m[4:5]) + m[3:4]).astype(BF16)


def _merge(x9, mod9, h9, gdn_f, gdn_b, ret_o, mla_o, gdn_nw, ret_nw, wg, wa, wb, wc, wo, n2, nslab):
    S9, N, D = x9.shape
    tm = min(256, N)
    W = HEADS * HEAD_DIM
    row = lambda w: pl.BlockSpec((1, tm, w), lambda b, i: (b, i, 0))
    full = lambda a: pl.BlockSpec(a.shape, lambda b, i: (0,) * a.ndim, pipeline_mode=pl.Buffered(1))
    gnw = jnp.tile(gdn_nw, HEADS).reshape(1, W)
    rnw = ret_nw.reshape(1, W)
    n2 = n2.reshape(1, D)
    return pl.pallas_call(
        _merge_kernel,
        grid=(nslab, N // tm),
        in_specs=[row(D), pl.BlockSpec((1, 6, D), lambda b, i: (b, 0, 0)), row(D),
                  row(W), row(W), row(W), row(W),
                  full(gnw), full(rnw), full(wg), full(wa), full(wb), full(wc), full(wo), full(n2)],
        out_specs=[row(D), row(D)],
        out_shape=[jax.ShapeDtypeStruct((nslab, N, D), F32), jax.ShapeDtypeStruct((nslab, N, D), BF16)],
        compiler_params=_cparams(("parallel", "parallel")),
        name="merge",
    )(x9, mod9, h9, gdn_f, gdn_b, ret_o, mla_o, gnw, rnw, wg, wa, wb, wc, wo, n2)


FFN_CHUNK = 256
FFN_LOOKAHEAD = 3


def _ffn_kernel(h_ref, hp_ref, hn_ref, x_ref, mod_ref, wup_ref, cw_ref, cb_ref, wd_ref, nw_ref, nmod_ref,
                *out_refs, nlat, seq_lat, seq_ctx, final):
    b, i = pl.program_id(0), pl.program_id(1)
    tm = h_ref.shape[1]
    FF = wd_ref.shape[0]
    nch = FF // FFN_CHUNK
    seq_len, is_ctx = _slab_seq(b, nlat, seq_lat, seq_ctx)
    he = _rows_with_halo(h_ref, hp_ref, hn_ref, i * tm, seq_len)

    def up(c):
        g = slice(c * FFN_CHUNK, (c + 1) * FFN_CHUNK)
        v = slice(FF + c * FFN_CHUNK, FF + (c + 1) * FFN_CHUNK)
        return (jnp.dot(he, wup_ref[:, g], preferred_element_type=F32),
                jnp.dot(he, wup_ref[:, v], preferred_element_type=F32))

    def act(c, ug, uv):
        g = slice(c * FFN_CHUNK, (c + 1) * FFN_CHUNK)
        v = slice(FF + c * FFN_CHUNK, FF + (c + 1) * FFN_CHUNK)
        gate = _conv3_ext(ug, cw_ref[:, g], seq_ctx, is_ctx) + cb_ref[:, g]
        val = _conv3_ext(uv, cw_ref[:, v], seq_ctx, is_ctx) + cb_ref[:, v]
        return (_silu(gate) * val).astype(BF16)

    ups = [up(c) for c in range(min(FFN_LOOKAHEAD, nch))]
    acc = None
    for c in range(nch):
        if c + FFN_LOOKAHEAD < nch:
            ups.append(up(c + FFN_LOOKAHEAD))
        part = jnp.dot(act(c, *ups[c]), wd_ref[c * FFN_CHUNK:(c + 1) * FFN_CHUNK, :], preferred_element_type=F32)
        ups[c] = None
        acc = part if acc is None else acc + part

    m = mod_ref[0]
    x2 = x_ref[0] + m[5:6] * acc
    if final:
        out_refs[0][0] = _rms(x2, nw_ref[...])
    else:
        nm = nmod_ref[0]
        out_refs[0][0] = x2
        out_refs[1][0] = (_rms(x2, nw_ref[...]) * (1.0 + nm[1:2]) + nm[0:1]).astype(BF16)


def _ffn(h2, x1, mod9, w_up, conv_w, conv_b, w_down, next_nw, next_mod9, nslab, nlat, seq_ctx, final):
    S9, N, D = x1.shape
    FF = w_down.shape[0]
    tm = min(512, N)
    row = lambda: pl.BlockSpec((1, tm, D), lambda b, i: (b, i, 0))
    modspec = pl.BlockSpec((1, 6, D), lambda b, i: (b, 0, 0))
    resident = lambda a: pl.BlockSpec(a.shape, lambda b, i: (0,) * a.ndim, pipeline_mode=pl.Buffered(1))
    cb = conv_b.reshape(1, 2 * FF)
    nw = next_nw.reshape(1, D)
    if final:
        out_specs = [row()]
        out_shape = [jax.ShapeDtypeStruct((nslab, N, D), F32)]
    else:
        out_specs = [row(), row()]
        out_shape = [jax.ShapeDtypeStruct((S9, N, D), F32), jax.ShapeDtypeStruct((S9, N, D), BF16)]
    return pl.pallas_call(
        functools.partial(_ffn_kernel, nlat=nlat, seq_lat=N, seq_ctx=seq_ctx, final=final),
        grid=(nslab, N // tm),
        in_specs=_halo_specs(tm, N, D) + [
            row(), modspec, resident(w_up), resident(conv_w), resident(cb), resident(w_down),
            pl.BlockSpec((1, D), lambda b, i: (0, 0)), modspec],
        out_specs=out_specs,
        out_shape=out_shape,
        compiler_params=_cparams(("parallel", "parallel")),
        name="ffn_final" if final else "ffn",
    )(h2, h2, h2, x1, mod9, w_up, conv_w, cb, w_down, nw, next_mod9)


def _axial_angles(n, d):
    rows = n // GRID_W
    r = jnp.repeat(jnp.arange(rows, dtype=F32), GRID_W)
    col = jnp.tile(jnp.arange(GRID_W, dtype=F32), rows)
    quarter = d // 4
    inv = ROPE_BASE ** (-jnp.arange(quarter, dtype=F32) / quarter)
    return jnp.concatenate([r[:, None] * inv, col[:, None] * inv], axis=-1)


def _rope_tables(n):
    a_r = _axial_angles(n, HEAD_DIM)
    cr, sr = jnp.cos(a_r), jnp.sin(a_r)
    cos_ret = jnp.stack([jnp.concatenate([cr, cr], -1), jnp.ones((n, HEAD_DIM), F32)])
    sin_ret = jnp.stack([jnp.concatenate([-sr, sr], -1), jnp.zeros((n, HEAD_DIM), F32)])
    a_m = _axial_angles(n, MLA_ROPE)
    cm, sm = jnp.cos(a_m), jnp.sin(a_m)
    cos_k = jnp.concatenate([cm, cm], -1)
    sin_k = jnp.concatenate([sm, sm], -1)
    pad = MLA_HEAD_PAD - MLA_NOPE - MLA_ROPE
    cos_q = jnp.concatenate([jnp.ones((n, MLA_NOPE), F32), cos_k, jnp.zeros((n, pad), F32)], -1)
    sin_q = jnp.concatenate([jnp.zeros((n, MLA_NOPE), F32), sin_k, jnp.zeros((n, pad), F32)], -1)
    ident_q = jnp.concatenate([jnp.ones((n, MLA_NOPE + MLA_ROPE), F32), jnp.zeros((n, pad), F32)], -1)
    return (cos_ret, sin_ret,
            jnp.stack([cos_q, ident_q]), jnp.stack([sin_q, jnp.zeros_like(sin_q)]),
            jnp.stack([cos_k, jnp.ones_like(cos_k)]), jnp.stack([sin_k, jnp.zeros_like(sin_k)]))


def _rot_half_cols(w):
    half = w.shape[-1] // 2
    return jnp.concatenate([-w[..., half:], w[..., :half]], axis=-1)


def kernel(x, c, ctx, c_ctx, ada_w, ada_b, norm1_w, w_in, gdn_conv_w, gdn_A_log, gdn_dt_bias, gdn_norm_w, mla_q_norm, mla_w_uq, mla_kv_norm, mla_w_ukv, ret_norm_w, w_br_gdn, w_br_mla, w_br_ret, w_out, norm2_w, ffn_w_up, ffn_conv_w, ffn_conv_b, ffn_w_down, final_norm_w):
    B, N, D = x.shape
    M = ctx.shape[1]
    L = ada_w.shape[0]
    W = HEADS * HEAD_DIM
    assert B * M == N and N % SEQ_BLOCK == 0 and M % SEQ_BLOCK == 0
    assert N & (N - 1) == 0 and M & (M - 1) == 0 and D == 1024

    x9 = jnp.concatenate([x, ctx.reshape(1, N, D)], axis=0)
    cc = jnp.zeros((16, D), F32).at[:B].set(c).at[B].set(c_ctx)
    mod = _modulation(cc, ada_w, ada_b)[:, :B + 1].reshape(L, B + 1, 6, D)
    cos_ret, sin_ret, cos_q, sin_q, cos_k, sin_k = _rope_tables(N)

    h = _norm_mod(x9, mod[0], norm1_w[0])
    out = None
    for l in range(L):
        last = l == L - 1
        nslab = B if last else B + 1
        wi = w_in[l]
        o_qkv, o_z, o_ab = 3 * W, 4 * W, 4 * W + 16
        o_cq = o_ab
        o_ckv = o_cq + MLA_Q_RANK
        o_kr = o_ckv + MLA_KV_RANK
        o_ret = o_kr + MLA_ROPE
        o_gate = o_ret + 4 * W
        w_kr = wi[:, o_kr:o_ret]
        w_ab = wi[:, o_z:o_ab]
        zc = lambda n: jnp.zeros((D, n), F32)
        w_mla = jnp.concatenate([wi[:, o_cq:o_kr], w_kr, zc(64), _rot_half_cols(w_kr), zc(64), w_ab, zc(112)],
                                axis=1).astype(BF16)
        assert w_mla.shape[1] == P_AB + 128
        w_gates = jnp.concatenate([wi[:, o_gate:], wi[:, o_qkv:o_z], wi[:, o_ret + 3 * W:o_gate]], axis=1).astype(BF16)

        qkv = _gdn_proj(h, wi[:, :o_qkv].astype(BF16), gdn_conv_w[l], B, M)
        ret_o = _ret_scan(h, wi[:, o_ret:o_ret + 3 * W].astype(BF16), cos_ret, sin_ret, B, M)

        wq = mla_w_uq[l].reshape(MLA_Q_RANK, HEADS, MLA_NOPE + MLA_ROPE)
        zq = jnp.zeros((MLA_Q_RANK, HEADS, MLA_HEAD_PAD - MLA_NOPE - MLA_ROPE), F32)
        wq_p = jnp.concatenate([wq, zq], -1).reshape(MLA_Q_RANK, HEADS * MLA_HEAD_PAD).astype(BF16)
        wq_r = jnp.concatenate([jnp.zeros_like(wq[..., :MLA_NOPE]), _rot_half_cols(wq[..., MLA_NOPE:]), zq],
                               -1).reshape(MLA_Q_RANK, HEADS * MLA_HEAD_PAD).astype(BF16)
        wkv = mla_w_ukv[l].reshape(MLA_KV_RANK, HEADS, 2 * HEAD_DIM)
        wk = wkv[..., :MLA_NOPE].reshape(MLA_KV_RANK, W).astype(BF16)
        wv = wkv[..., MLA_NOPE:].reshape(MLA_KV_RANK, W).astype(BF16)
        Q, K, V, ab, abT = _mla_prep(h, w_mla, w_ab.T.astype(BF16), mla_q_norm[l], wq_p,
                                     wq_r, mla_kv_norm[l], wk, wv, cos_q, sin_q, cos_k, sin_k, B)
        gdn_f, gdn_b = _gdn_scan(qkv, ab, abT, gdn_A_log[l], gdn_dt_bias[l], B, M)
        mla_o = _mla_attn(Q, K, V, B, M, with_ctx=not last)

        x1, h2 = _merge(x9, mod[l], h, gdn_f, gdn_b, ret_o, mla_o, gdn_norm_w[l], ret_norm_w[l], w_gates,
                        w_br_gdn[l].astype(BF16), w_br_mla[l].astype(BF16), w_br_ret[l].astype(BF16),
                        w_out[l].astype(BF16), norm2_w[l], nslab)
        ffn_args = (h2, x1, mod[l], ffn_w_up[l].astype(BF16), ffn_conv_w[l], ffn_conv_b[l],
                    ffn_w_down[l].astype(BF16))
        if last:
            (out,) = _ffn(*ffn_args, final_norm_w, mod[l], nslab, B, M, True)
        else:
            x9, h = _ffn(*ffn_args, norm1_w[l + 1], mod[l + 1], nslab, B, M, False)
    return out
```

```python
import functools
import math

import jax
import jax.numpy as jnp
from jax import lax
from jax.experimental import pallas as pl
from jax.experimental.pallas import tpu as pltpu

F32 = jnp.float32
BF16 = jnp.bfloat16

EPS = 1e-6
GRID_W = 64
ROPE_BASE = 10000.0
HEADS = 4
HEAD_DIM = 128
CHUNK = 64
MLA_Q_RANK = 384
MLA_KV_RANK = 256
MLA_NOPE = 128
MLA_ROPE = 64
MLA_SCALE = (MLA_NOPE + MLA_ROPE) ** -0.5
MLA_HEAD_PAD = 256
RET_DECAY_BASE = 5.0
RET_DIR_OFFSET = 0.5
QK_SCALE = HEAD_DIM ** -0.5
NEG_BIG = -1e30

P_KR = MLA_Q_RANK + MLA_KV_RANK
P_KR_ROT = P_KR + 128
P_AB = P_KR_ROT + 128

V7X_VMEM_LIMIT = 56 * 1024 * 1024
SEQ_BLOCK = 256
HALO = 16


def _cparams(sem):
    return pltpu.CompilerParams(dimension_semantics=sem, vmem_limit_bytes=V7X_VMEM_LIMIT)


def _bdot(a, b):
    return jnp.dot(a.astype(BF16), b.astype(BF16), preferred_element_type=F32)


def _bdot_nt(a, b):
    return lax.dot_general(a.astype(BF16), b.astype(BF16), (((1,), (1,)), ((), ())),
                           preferred_element_type=F32)


def _bdot_tn(a, b):
    return lax.dot_general(a.astype(BF16), b.astype(BF16), (((0,), (0,)), ((), ())),
                           preferred_element_type=F32)


def _rms(x, w):
    return x * lax.rsqrt(jnp.mean(x * x, axis=-1, keepdims=True) + EPS) * w


def _silu(x):
    return x * (1.0 / (1.0 + jnp.exp(-x)))


def _sigmoid(x):
    return 1.0 / (1.0 + jnp.exp(-x))


def _mod_kernel(c_ref, w_ref, b_ref, o_ref):
    a = _silu(c_ref[...])
    o_ref[0] = _bdot(a, w_ref[0]) + b_ref[0]


def _modulation(cc, ada_w, ada_b):
    L, D, D6 = ada_w.shape
    R = cc.shape[0]
    tn = 1536
    return pl.pallas_call(
        _mod_kernel,
        grid=(L, D6 // tn),
        in_specs=[pl.BlockSpec((R, D), lambda l, j: (0, 0)),
                  pl.BlockSpec((1, D, tn), lambda l, j: (l, 0, j)),
                  pl.BlockSpec((1, 1, tn), lambda l, j: (l, 0, j))],
        out_specs=pl.BlockSpec((1, R, tn), lambda l, j: (l, 0, j)),
        out_shape=jax.ShapeDtypeStruct((L, R, D6), F32),
        compiler_params=_cparams(("parallel", "parallel")),
        name="modulation",
    )(cc, ada_w, ada_b.reshape(L, 1, D6))


def _norm_mod_kernel(x_ref, mod_ref, nw_ref, h_ref):
    m = mod_ref[0]
    h = _rms(x_ref[0], nw_ref[...]) * (1.0 + m[1:2]) + m[0:1]
    h_ref[0] = h.astype(BF16)


def _norm_mod(x9, mod9, nw):
    S, N, D = x9.shape
    tm = min(512, N)
    return pl.pallas_call(
        _norm_mod_kernel,
        grid=(S, N // tm),
        in_specs=[pl.BlockSpec((1, tm, D), lambda b, i: (b, i, 0)),
                  pl.BlockSpec((1, 6, D), lambda b, i: (b, 0, 0)),
                  pl.BlockSpec((1, D), lambda b, i: (0, 0))],
        out_specs=pl.BlockSpec((1, tm, D), lambda b, i: (b, i, 0)),
        out_shape=jax.ShapeDtypeStruct((S, N, D), BF16),
        compiler_params=_cparams(("parallel", "parallel")),
        name="norm_mod",
    )(x9, mod9, nw.reshape(1, D))


def _rows_with_halo(h_ref, hp_ref, hn_ref, row0, seq_len):
    tm = h_ref.shape[1]
    hp, hn = hp_ref[0], hn_ref[0]
    hp = jnp.where((row0 & (seq_len - 1)) == 0, jnp.zeros_like(hp), hp)
    hn = jnp.where(((row0 + tm) & (seq_len - 1)) == 0, jnp.zeros_like(hn), hn)
    return jnp.concatenate([hp, h_ref[0], hn], axis=0)


def _conv3_ext(ue, cw, short_len, is_short):
    n = ue.shape[0]
    tm = n - 2 * HALO
    mid = slice(HALO, n - HALO)
    y = cw[0:1] * pltpu.roll(ue, 1, 0)[mid] + cw[1:2] * ue[mid] + cw[2:3] * pltpu.roll(ue, n - 1, 0)[mid]
    if short_len >= tm:
        return y
    rows = lax.broadcasted_iota(jnp.int32, (2 * 8, 1), 0)
    pieces, at = [], 0
    for edge in range(short_len, tm, short_len):
        blk = ue[HALO + edge - 8:HALO + edge + 8]
        fix = (jnp.where(rows == 7, is_short, 0.0) * (cw[2:3] * blk[8:9])
               + jnp.where(rows == 8, is_short, 0.0) * (cw[0:1] * blk[7:8]))
        pieces += [y[at:edge - 8], y[edge - 8:edge + 8] - fix]
        at = edge + 8
    return jnp.concatenate(pieces + [y[at:]], axis=0)


def _halo_specs(tm, N, D):
    nb = tm // HALO
    last = N // HALO - 1
    return [pl.BlockSpec((1, tm, D), lambda b, i, *_: (b, i, 0)),
            pl.BlockSpec((1, HALO, D), lambda b, i, *_: (b, jnp.maximum(i * nb - 1, 0), 0)),
            pl.BlockSpec((1, HALO, D), lambda b, i, *_: (b, jnp.minimum((i + 1) * nb, last), 0))]


def _slab_seq(b, nlat, seq_lat, seq_ctx):
    is_ctx = b == nlat
    return jnp.where(is_ctx, seq_ctx, seq_lat), jnp.where(is_ctx, 1.0, 0.0)


GDN_PROJ_CHUNK = 2 * HEAD_DIM
GDN_PROJ_LOOKAHEAD = 2


def _gdn_proj_kernel(h_ref, hp_ref, hn_ref, w_ref, cw_ref, o_ref, *, nlat, seq_lat, seq_ctx):
    b, i = pl.program_id(0), pl.program_id(1)
    tm = h_ref.shape[1]
    W = HEADS * HEAD_DIM
    nch = w_ref.shape[1] // GDN_PROJ_CHUNK
    seq_len, is_ctx = _slab_seq(b, nlat, seq_lat, seq_ctx)
    he = _rows_with_halo(h_ref, hp_ref, hn_ref, i * tm, seq_len)
    cols = lambda c: slice(c * GDN_PROJ_CHUNK, (c + 1) * GDN_PROJ_CHUNK)
    up = lambda c: jnp.dot(he, w_ref[:, cols(c)], preferred_element_type=F32)

    def finish(c, ue):
        y = _silu(_conv3_ext(ue, cw_ref[:, cols(c)], seq_ctx, is_ctx))
        part = c * GDN_PROJ_CHUNK // W
        for h in range(GDN_PROJ_CHUNK // HEAD_DIM):
            yh = y[:, h * HEAD_DIM:(h + 1) * HEAD_DIM]
            if part < 2:
                r = lax.rsqrt(jnp.sum(yh * yh, axis=-1, keepdims=True) + EPS)
                yh = yh * r * QK_SCALE if part == 0 else yh * r
            lo = c * GDN_PROJ_CHUNK + h * HEAD_DIM
            o_ref[0, :, lo:lo + HEAD_DIM] = yh

    ups = [up(c) for c in range(min(GDN_PROJ_LOOKAHEAD, nch))]
    for c in range(nch):
        if c + GDN_PROJ_LOOKAHEAD < nch:
            ups.append(up(c + GDN_PROJ_LOOKAHEAD))
        finish(c, ups[c])
        ups[c] = None


def _gdn_proj(h9, w_qkv, conv_w, nlat, seq_ctx):
    S, N, D = h9.shape
    C = w_qkv.shape[1]
    tm = min(1024, N)
    resident = lambda a: pl.BlockSpec(a.shape, lambda b, i: (0,) * a.ndim, pipeline_mode=pl.Buffered(1))
    return pl.pallas_call(
        functools.partial(_gdn_proj_kernel, nlat=nlat, seq_lat=N, seq_ctx=seq_ctx),
        grid=(S, N // tm),
        in_specs=_halo_specs(tm, N, D) + [resident(w_qkv), resident(conv_w)],
        out_specs=pl.BlockSpec((1, tm, C), lambda b, i: (b, i, 0)),
        out_shape=jax.ShapeDtypeStruct((S, N, C), F32),
        compiler_params=_cparams(("parallel", "parallel")),
        name="gdn_proj",
    )(h9, h9, h9, w_qkv, conv_w)


def _seq_maps(nlat, n_ctx_blk, n_lat_blk):
    def fwd(b, s):
        is_ctx = s < n_ctx_blk
        return jnp.where(is_ctx, nlat, b), jnp.where(is_ctx, b * n_ctx_blk + s, s - n_ctx_blk)

    def bwd(b, s):
        is_ctx = s < n_ctx_blk
        return (jnp.where(is_ctx, nlat, b),
                jnp.where(is_ctx, b * n_ctx_blk + n_ctx_blk - 1 - s, n_lat_blk - 1 - (s - n_ctx_blk)))

    return fwd, bwd


def _tri_masks(n, backward):
    r = lax.broadcasted_iota(jnp.int32, (n, n), 0)
    c = lax.broadcasted_iota(jnp.int32, (n, n), 1)
    if backward:
        return r <= c, r < c
    return r >= c, r > c


def _gdn_gates(backward, ab_ref, abt_ref, alog_c, dtb_c, alog_r, dtb_r):
    ab = ab_ref[0]
    abt = abt_ref[0]
    T = ab.shape[0]

    def softplus(x):
        return jnp.maximum(x, 0.0) + jnp.log1p(jnp.exp(-jnp.abs(x)))

    g_c = -jnp.exp(alog_c) * softplus(ab + dtb_c)
    g_r = -jnp.exp(alog_r) * softplus(abt + dtb_r)
    beta_c = _sigmoid(ab)
    ri = lax.broadcasted_iota(jnp.int32, (T, T), 0)
    ci = lax.broadcasted_iota(jnp.int32, (T, T), 1)
    same = (ri // CHUNK) == (ci // CHUNK)
    cs = jnp.where(same & ((ci >= ri) if backward else (ci <= ri)), 1.0, 0.0).astype(BF16)

    def split3(g):
        hi = g.astype(BF16)
        r1 = g - hi.astype(F32)
        mid = r1.astype(BF16)
        lo = (r1 - mid.astype(F32)).astype(BF16)
        return hi, mid, lo

    pc = jnp.dot(cs, jnp.concatenate(split3(g_c), axis=1), preferred_element_type=F32)
    gcum_c = pc[:, :128] + pc[:, 128:256] + pc[:, 256:]
    pr = lax.dot_general(jnp.concatenate(split3(g_r), axis=0), cs, (((1,), (1,)), ((), ())),
                         preferred_element_type=F32)
    gcum_r = pr[:16] + pr[16:32] + pr[32:]
    return gcum_c, gcum_r, jnp.exp(gcum_c), beta_c


def _gdn_kernel(qf, kf, vf, abf, abtf, qb, kb, vb, abb, abtb, alog_c, dtb_c, alog_r, dtb_r,
                of_ref, ob_ref, sf_ref, sb_ref):
    @pl.when(pl.program_id(1) == 0)
    def _():
        sf_ref[...] = jnp.zeros_like(sf_ref)
        sb_ref[...] = jnp.zeros_like(sb_ref)

    ac, dc, ar, dr = alog_c[...], dtb_c[...], alog_r[...], dtb_r[...]
    C = CHUNK
    nchunk = qf.shape[1] // C
    eye = jnp.where(lax.broadcasted_iota(jnp.int32, (C, C), 0)
                    == lax.broadcasted_iota(jnp.int32, (C, C), 1), 1.0, 0.0)
    dirs = []
    for d, (q_ref, k_ref, v_ref, ab_ref, abt_ref, o_ref, s_ref) in enumerate(
            ((qf, kf, vf, abf, abtf, of_ref, sf_ref), (qb, kb, vb, abb, abtb, ob_ref, sb_ref))):
        backward = d == 1
        gates = _gdn_gates(backward, ab_ref, abt_ref, ac, dc, ar, dr)
        dirs.append(dict(d=d, backward=backward, q_ref=q_ref, k_ref=k_ref, v_ref=v_ref, o_ref=o_ref,
                         s_ref=s_ref, gates=gates, masks=_tri_masks(C, backward)))

    units = []
    for pos in range(nchunk):
        for dr_ in dirs:
            c = nchunk - 1 - pos if dr_["backward"] else pos
            for h in range(HEADS):
                units.append(dict(dr_, h=h, c=c, pos=pos))

    for u in units:
        gcum_c, gcum_r, eg_c, beta_c = u["gates"]
        incl, strict = u["masks"]
        d, h, c = u["d"], u["h"], u["c"]
        ia, ib = 4 * d + h, 8 + 4 * d + h
        rs = slice(c * C, (c + 1) * C)
        hs = slice(h * HEAD_DIM, (h + 1) * HEAD_DIM)
        u["rs"], u["hs"] = rs, hs
        q = u["q_ref"][0, rs, hs]
        k = u["k_ref"][0, rs, hs]
        v = u["v_ref"][0, rs, hs]
        gc = gcum_c[rs, ia:ia + 1]
        gr = gcum_r[ia:ia + 1, rs]
        egc = eg_c[rs, ia:ia + 1]
        beta = beta_c[rs, ib:ib + 1]
        last = c * C if u["backward"] else (c + 1) * C - 1
        gtot = gcum_c[last:last + 1, ia:ia + 1]
        decay = jnp.exp(jnp.where(incl, gc - gr, NEG_BIG))
        kbeta = k * beta
        qk = _bdot_nt(jnp.concatenate([q, kbeta], axis=0), k)
        u["attn"] = (qk[:C] * decay).astype(BF16)
        x = -(qk[C:] * jnp.where(strict, decay, 0.0))
        u["p"] = x
        u["t"] = eye + x
        u["rhs"] = jnp.concatenate([v * beta, kbeta * egc], axis=1).astype(BF16)
        u["qd"] = (q * egc).astype(BF16)
        u["kd"] = (k * jnp.exp(gtot - gc)).astype(BF16)
        u["cdec"] = jnp.exp(gtot)

    for u in units:
        u["p"] = _bdot(u["p"], u["p"])
    for _ in range(int(math.log2(C)) - 2):
        for u in units:
            r = _bdot(jnp.concatenate([u["p"], u["t"]], axis=0), u["p"])
            u["t"] = u["t"] + r[C:]
            u["p"] = r[:C]
    for u in units:
        u["t"] = u["t"] + _bdot(u["t"], u["p"])

    for u in units:
        u["sol"] = _bdot(u["t"], u["rhs"])

    S = {(dr_["d"], h): dr_["s_ref"][h] for dr_ in dirs for h in range(HEADS)}
    for pos in range(nchunk):
        cur = [u for u in units if u["pos"] == pos]
        for u in cur:
            lhs = jnp.concatenate([u["sol"][:, HEAD_DIM:].astype(BF16), u["qd"]], axis=0)
            u["ws"] = _bdot(lhs, S[u["d"], u["h"]])
        for u in cur:
            key = (u["d"], u["h"])
            v_new = (u["sol"][:, :HEAD_DIM] - u["ws"][:C]).astype(BF16)
            u["o_ref"][0, u["rs"], u["hs"]] = u["ws"][C:] + _bdot(u["attn"], v_new)
            S[key] = u["cdec"] * S[key] + _bdot_tn(u["kd"], v_new)
    for dr_ in dirs:
        for h in range(HEADS):
            dr_["s_ref"][h] = S[dr_["d"], h]


def _gdn_scan(qkv, ab, abT, A_log, dt_bias, nlat, seq_ctx):
    S9, N, _ = qkv.shape
    T = SEQ_BLOCK
    ncb, nlb = seq_ctx // T, N // T
    fwd, bwd = _seq_maps(nlat, ncb, nlb)
    W = HEADS * HEAD_DIM

    def specs(m):
        return [pl.BlockSpec((1, T, W), lambda b, s, m=m: (*m(b, s), 0)),
                pl.BlockSpec((1, T, W), lambda b, s, m=m: (*m(b, s), 1)),
                pl.BlockSpec((1, T, W), lambda b, s, m=m: (*m(b, s), 2)),
                pl.BlockSpec((1, T, 128), lambda b, s, m=m: (*m(b, s), 0)),
                pl.BlockSpec((1, 16, T), lambda b, s, m=m: (m(b, s)[0], 0, m(b, s)[1]))]

    small = lambda shape: pl.BlockSpec(shape, lambda b, s: (0, 0))
    alog = A_log.reshape(-1)
    dtb = dt_bias.reshape(-1)
    pad_c = lambda t: jnp.zeros((1, 128), F32).at[0, :8].set(t)
    pad_r = lambda t: jnp.zeros((16, 1), F32).at[:8, 0].set(t)
    out_sds = jax.ShapeDtypeStruct((S9, N, W), F32)
    return pl.pallas_call(
        _gdn_kernel,
        grid=(nlat, ncb + nlb),
        in_specs=specs(fwd) + specs(bwd) + [small((1, 128)), small((1, 128)), small((16, 1)), small((16, 1))],
        out_specs=[pl.BlockSpec((1, T, W), lambda b, s: (*fwd(b, s), 0)),
                   pl.BlockSpec((1, T, W), lambda b, s: (*bwd(b, s), 0))],
        out_shape=[out_sds, out_sds],
        scratch_shapes=[pltpu.VMEM((HEADS, HEAD_DIM, HEAD_DIM), F32),
                        pltpu.VMEM((HEADS, HEAD_DIM, HEAD_DIM), F32)],
        compiler_params=_cparams(("parallel", "arbitrary")),
        name="gdn_scan",
    )(qkv, qkv, qkv, ab, abT, qkv, qkv, qkv, ab, abT, pad_c(alog), pad_c(dtb), pad_r(alog), pad_r(dtb))


def _ret_log_gamma(d, h):
    return math.log1p(-(2.0 ** (-(RET_DECAY_BASE + h + RET_DIR_OFFSET * d))))


def _ret_kernel(h_ref, w_ref, cos_ref, sin_ref, o_ref, sf_ref, sb_ref, sb_all_ref, k_keep_ref, v_keep_ref,
                decay_ref, *, n_ctx_blk, n_lat_blk):
    sweep, s = pl.program_id(1), pl.program_id(2)
    W = HEADS * HEAD_DIM
    C = h_ref.shape[1]
    half = HEAD_DIM // 2
    hrow = h_ref[0]
    cos, sin = cos_ref[0], sin_ref[0]
    pos = lax.broadcasted_iota(jnp.int32, (C, 1), 0).astype(F32)
    proj = lambda j: jnp.dot(hrow, w_ref[:, j * W:(j + 1) * W], preferred_element_type=F32)
    rope = lambda t: t * cos + pltpu.roll(t, half, 1) * sin
    head_slices = [slice(h * HEAD_DIM, (h + 1) * HEAD_DIM) for h in range(HEADS)]

    @pl.when(sweep == 0)
    def _():
        @pl.when(s == 0)
        def _():
            sb_ref[...] = jnp.zeros_like(sb_ref)

        k_all, v_all = proj(1), proj(2).astype(BF16)
        v_keep_ref[s] = v_all
        for h, hs in enumerate(head_slices):
            lg = _ret_log_gamma(1, h)
            k = rope(k_all[:, hs])
            k_keep_ref[s, :, hs] = k
            S = sb_ref[h]
            sb_all_ref[s * HEADS + h] = S
            sb_ref[h] = math.exp(lg * C) * S + _bdot_tn(k * jnp.exp(lg * pos), v_all[:, hs])

    @pl.when(sweep == 1)
    def _():
        @pl.when(s == 0)
        def _():
            sf_ref[...] = jnp.zeros_like(sf_ref)
            r = lax.broadcasted_iota(jnp.int32, (C, C), 0)
            c = lax.broadcasted_iota(jnp.int32, (C, C), 1)
            dist = jnp.abs(r - c).astype(F32)
            for h in range(HEADS):
                lf, lb = _ret_log_gamma(0, h), _ret_log_gamma(1, h)
                decay_ref[h] = (jnp.where(r >= c, jnp.exp(lf * dist), 0.0)
                                + jnp.where(r <= c, jnp.exp(lb * dist), 0.0))

        seen = jnp.where(s < n_ctx_blk, n_ctx_blk - 1 - s, 2 * n_ctx_blk + n_lat_blk - 1 - s)
        q_all = proj(0)
        units = []
        for h, hs in enumerate(head_slices):
            k = k_keep_ref[seen, :, hs]
            v = v_keep_ref[seen, :, hs]
            lf, lb = _ret_log_gamma(0, h), _ret_log_gamma(1, h)
            q = rope(q_all[:, hs]) * QK_SCALE
            Sf = sf_ref[h]
            q2 = jnp.concatenate([q * jnp.exp(lf * (pos + 1.0)), q * jnp.exp(lb * (C - pos))], axis=1)
            S2 = jnp.concatenate([Sf, sb_all_ref[seen * HEADS + h]], axis=0)
            units.append((h, hs, v, _bdot_nt(q, k), _bdot(q2, S2),
                          math.exp(lf * C) * Sf + _bdot_tn(k * jnp.exp(lf * (C - 1 - pos)), v)))
        for h, hs, v, qk, inter, s_new in units:
            o_ref[0, :, hs] = _bdot((qk * decay_ref[h]).astype(BF16), v) + inter
            sf_ref[h] = s_new


def _ret_scan(h9, w_ret, cos2, sin2, nlat, seq_ctx):
    S9, N, D = h9.shape
    T = SEQ_BLOCK
    ncb, nlb = seq_ctx // T, N // T
    fwd, bwd = _seq_maps(nlat, ncb, nlb)
    W = HEADS * HEAD_DIM

    def blk(b, sweep, s):
        f, w = fwd(b, s), bwd(b, s)
        return jnp.where(sweep == 0, w[0], f[0]), jnp.where(sweep == 0, w[1], f[1])

    def tab(b, sweep, s):
        slab, i = blk(b, sweep, s)
        return jnp.where(slab == nlat, 1, 0), jnp.where(slab == nlat, 0, i), 0

    return pl.pallas_call(
        functools.partial(_ret_kernel, n_ctx_blk=ncb, n_lat_blk=nlb),
        grid=(nlat, 2, ncb + nlb),
        in_specs=[pl.BlockSpec((1, T, D), lambda b, sweep, s: (*blk(b, sweep, s), 0)),
                  pl.BlockSpec(w_ret.shape, lambda b, sweep, s: (0, 0), pipeline_mode=pl.Buffered(1)),
                  pl.BlockSpec((1, T, HEAD_DIM), tab), pl.BlockSpec((1, T, HEAD_DIM), tab)],
        out_specs=pl.BlockSpec((1, T, W), lambda b, sweep, s: (*fwd(b, jnp.where(sweep == 0, 0, s)), 0)),
        out_shape=jax.ShapeDtypeStruct((S9, N, W), F32),
        scratch_shapes=[pltpu.VMEM((HEADS, HEAD_DIM, HEAD_DIM), F32),
                        pltpu.VMEM((HEADS, HEAD_DIM, HEAD_DIM), F32),
                        pltpu.VMEM(((ncb + nlb) * HEADS, HEAD_DIM, HEAD_DIM), F32),
                        pltpu.VMEM((ncb + nlb, T, W), F32),
                        pltpu.VMEM((ncb + nlb, T, W), BF16),
                        pltpu.VMEM((HEADS, T, T), F32)],
        compiler_params=_cparams(("parallel", "arbitrary", "arbitrary")),
        name="ret_scan",
    )(h9, w_ret, cos2, sin2)


def _mla_prep_kernel(h_ref, wm_ref, wabt_ref, qn_ref, wq_ref, wqr_ref, kvn_ref, wk_ref, wv_ref,
                     cq_ref, sq_ref, ck_ref, sk_ref, q_ref, k_ref, v_ref, ab_ref, abt_ref):
    hrow = h_ref[0]
    p = jnp.dot(hrow, wm_ref[...], preferred_element_type=F32)
    ab_ref[0] = p[:, P_AB:P_AB + 128]
    abt_ref[0] = lax.dot_general(wabt_ref[...], hrow, (((1,), (1,)), ((), ())), preferred_element_type=F32)
    cq = _rms(p[:, :MLA_Q_RANK], qn_ref[...]).astype(BF16)
    ckv = _rms(p[:, MLA_Q_RANK:MLA_Q_RANK + MLA_KV_RANK], kvn_ref[...]).astype(BF16)
    cosq, sinq = cq_ref[0], sq_ref[0]
    q = jnp.dot(cq, wq_ref[...], preferred_element_type=F32)
    qr = jnp.dot(cq, wqr_ref[...], preferred_element_type=F32)
    for h in range(HEADS):
        hs = slice(h * MLA_HEAD_PAD, (h + 1) * MLA_HEAD_PAD)
        q_ref[0, :, hs] = (q[:, hs] * cosq + qr[:, hs] * sinq).astype(BF16)
    kn = jnp.dot(ckv, wk_ref[...], preferred_element_type=F32)
    v_ref[0] = jnp.dot(ckv, wv_ref[...], preferred_element_type=F32).astype(BF16)
    krope = (p[:, P_KR:P_KR + MLA_ROPE] * ck_ref[0] + p[:, P_KR_ROT:P_KR_ROT + MLA_ROPE] * sk_ref[0]).astype(BF16)
    zpad = jnp.zeros((p.shape[0], MLA_HEAD_PAD - MLA_NOPE - MLA_ROPE), BF16)
    for h in range(HEADS):
        o = h * MLA_HEAD_PAD
        k_ref[0, :, o:o + MLA_NOPE] = kn[:, h * MLA_NOPE:(h + 1) * MLA_NOPE].astype(BF16)
        k_ref[0, :, o + MLA_NOPE:o + MLA_NOPE + MLA_ROPE] = krope
        k_ref[0, :, o + MLA_NOPE + MLA_ROPE:o + MLA_HEAD_PAD] = zpad


def _mla_prep(h9, w_mla, w_abt, q_norm, wq, wq_rot, kv_norm, wk, wv, cosq, sinq, cosk, sink, nlat):
    S9, N, D = h9.shape
    tm = min(512, N)
    QW = HEADS * MLA_HEAD_PAD
    VW = HEADS * HEAD_DIM
    full = lambda a: pl.BlockSpec(a.shape, lambda b, i: (0,) * a.ndim)
    tab = lambda w: pl.BlockSpec((1, tm, w), lambda b, i: (jnp.where(b == nlat, 1, 0), jnp.where(b == nlat, 0, i), 0))
    row = lambda w: pl.BlockSpec((1, tm, w), lambda b, i: (b, i, 0))
    qn = q_norm.reshape(1, -1)
    kvn = kv_norm.reshape(1, -1)
    return pl.pallas_call(
        _mla_prep_kernel,
        grid=(S9, N // tm),
        in_specs=[row(D), full(w_mla), full(w_abt),
                  full(qn), full(wq), full(wq_rot), full(kvn), full(wk), full(wv),
                  tab(MLA_HEAD_PAD), tab(MLA_HEAD_PAD), tab(MLA_ROPE), tab(MLA_ROPE)],
        out_specs=[row(QW), row(QW), row(VW), row(128),
                   pl.BlockSpec((1, 16, tm), lambda b, i: (b, 0, i))],
        out_shape=[jax.ShapeDtypeStruct((S9, N, QW), BF16),
                   jax.ShapeDtypeStruct((S9, N, QW), BF16),
                   jax.ShapeDtypeStruct((S9, N, VW), BF16),
                   jax.ShapeDtypeStruct((S9, N, 128), F32),
                   jax.ShapeDtypeStruct((S9, 16, N), F32)],
        compiler_params=_cparams(("parallel", "parallel")),
        name="mla_prep",
    )(h9, w_mla, w_abt, qn, wq, wq_rot, kvn, wk, wv, cosq, sinq, cosk, sink)


MLA_SCALE_LOG2E = MLA_SCALE * math.log2(math.e)


def _mla_attn_kernel(q_ref, kl_ref, kc_ref, vl_ref, vc_ref, o_ref, *, n_lat_tiles):
    nt = lambda a, b: lax.dot_general(a, b, (((1,), (1,)), ((), ())), preferred_element_type=F32)

    def heads(latent_keys):
        def scores(h):
            hs = slice(h * MLA_HEAD_PAD, (h + 1) * MLA_HEAD_PAD)
            q = q_ref[0, :, hs]
            s2 = nt(q, kc_ref[0, :, hs])
            return (nt(q, kl_ref[0, :, hs]), s2) if latent_keys else (s2,)

        def probs(*ss):
            m = functools.reduce(jnp.maximum, [jnp.max(s, axis=-1, keepdims=True) for s in ss])
            mc = m * MLA_SCALE_LOG2E
            ps = [jnp.exp2(s * MLA_SCALE_LOG2E - mc) for s in ss]
            l = functools.reduce(jnp.add, [jnp.sum(p, axis=-1, keepdims=True) for p in ps])
            return [p.astype(BF16) for p in ps], l

        def out(h, ps, l):
            hs = slice(h * HEAD_DIM, (h + 1) * HEAD_DIM)
            o = jnp.dot(ps[-1], vc_ref[0, :, hs], preferred_element_type=F32)
            if latent_keys:
                o = o + jnp.dot(ps[0], vl_ref[0, :, hs], preferred_element_type=F32)
            o_ref[0, :, hs] = (o / l).astype(BF16)

        s, p = {}, {}
        for step in range(HEADS + 2):
            if step < HEADS:
                s[step] = scores(step)
            if 0 <= step - 1 < HEADS:
                p[step - 1] = probs(*s.pop(step - 1))
            if 0 <= step - 2 < HEADS:
                out(step - 2, *p.pop(step - 2))

    i = pl.program_id(1)
    pl.when(i < n_lat_tiles)(lambda: heads(True))
    pl.when(i >= n_lat_tiles)(lambda: heads(False))


def _mla_attn(Q, K, V, nlat, seq_ctx, with_ctx):
    S9, N, QW = Q.shape
    VW = V.shape[2]
    M = seq_ctx
    tq = M
    nq = N // tq

    def qmap(b, i):
        is_ctx = i >= nq
        return jnp.where(is_ctx, nlat, b), jnp.where(is_ctx, b, i), 0

    return pl.pallas_call(
        functools.partial(_mla_attn_kernel, n_lat_tiles=nq),
        grid=(nlat, nq + 1 if with_ctx else nq),
        in_specs=[pl.BlockSpec((1, tq, QW), qmap),
                  pl.BlockSpec((1, N, QW), lambda b, i: (b, 0, 0)),
                  pl.BlockSpec((1, M, QW), lambda b, i: (nlat, b, 0)),
                  pl.BlockSpec((1, N, VW), lambda b, i: (b, 0, 0)),
                  pl.BlockSpec((1, M, VW), lambda b, i: (nlat, b, 0))],
        out_specs=pl.BlockSpec((1, tq, VW), qmap),
        out_shape=jax.ShapeDtypeStruct((S9 if with_ctx else nlat, N, VW), BF16),
        compiler_params=_cparams(("parallel", "arbitrary")),
        name="mla_attn",
    )(Q, K, K, V, V)


def _head_norm_gate(o, gate, w):
    parts = []
    for h in range(HEADS):
        hs = slice(h * HEAD_DIM, (h + 1) * HEAD_DIM)
        parts.append(_rms(o[:, hs], w[:, hs]))
    return (jnp.concatenate(parts, axis=1) * _silu(gate)).astype(BF16)


def _sigmoid_tanh(x):
    return 0.5 * jnp.tanh(0.5 * x) + 0.5


def _merge_kernel(x_ref, mod_ref, h_ref, gf_ref, gb_ref, ro_ref, mo_ref,
                  gnw_ref, rnw_ref, wg_ref, wa_ref, wb_ref, wc_ref, wo_ref, n2_ref, x1_ref, h2_ref):
    m = mod_ref[0]
    D = x_ref.shape[2]
    W = HEADS * HEAD_DIM
    hrow = h_ref[0]
    proj = lambda lo, n: jnp.dot(hrow, wg_ref[:, lo:lo + n], preferred_element_type=F32)
    zg = proj(3 * D, 2 * W)
    gates = [proj(k * D, D) for k in range(3)]
    oa = _head_norm_gate(gf_ref[0] + gb_ref[0], zg[:, :W], gnw_ref[...])
    oc = _head_norm_gate(ro_ref[0], zg[:, W:], rnw_ref[...])
    ya = jnp.dot(oa, wa_ref[...], preferred_element_type=F32)
    yb = jnp.dot(mo_ref[0], wb_ref[...], preferred_element_type=F32)
    yc = jnp.dot(oc, wc_ref[...], preferred_element_type=F32)
    mix = _sigmoid_tanh(gates[0]) * ya + _sigmoid_tanh(gates[1]) * yb + _sigmoid_tanh(gates[2]) * yc
    y = jnp.dot(mix.astype(BF16), wo_ref[...], preferred_element_type=F32)
    x1 = x_ref[0] + m[2:3] * y
    x1_ref[0] = x1
    h2_ref[0] = (_rms(x1, n2_ref[...]) * (1.0 + m[4:5]) + m[3:4]).astype(BF16)


def _merge(x9, mod9, h9, gdn_f, gdn_b, ret_o, mla_o, gdn_nw, ret_nw, wg, wa, wb, wc, wo, n2, nslab):
    S9, N, D = x9.shape
    tm = min(256, N)
    W = HEADS * HEAD_DIM
    row = lambda w: pl.BlockSpec((1, tm, w), lambda b, i: (b, i, 0))
    full = lambda a: pl.BlockSpec(a.shape, lambda b, i: (0,) * a.ndim, pipeline_mode=pl.Buffered(1))
    gnw = jnp.tile(gdn_nw, HEADS).reshape(1, W)
    rnw = ret_nw.reshape(1, W)
    n2 = n2.reshape(1, D)
    return pl.pallas_call(
        _merge_kernel,
        grid=(nslab, N // tm),
        in_specs=[row(D), pl.BlockSpec((1, 6, D), lambda b, i: (b, 0, 0)), row(D),
                  row(W), row(W), row(W), row(W),
                  full(gnw), full(rnw), full(wg), full(wa), full(wb), full(wc), full(wo), full(n2)],
        out_specs=[row(D), row(D)],
        out_shape=[jax.ShapeDtypeStruct((nslab, N, D), F32), jax.ShapeDtypeStruct((nslab, N, D), BF16)],
        compiler_params=_cparams(("parallel", "parallel")),
        name="merge",
    )(x9, mod9, h9, gdn_f, gdn_b, ret_o, mla_o, gnw, rnw, wg, wa, wb, wc, wo, n2)


FFN_CHUNK = 256
FFN_LOOKAHEAD = 3


def _ffn_kernel(h_ref, hp_ref, hn_ref, x_ref, mod_ref, wup_ref, cw_ref, cb_ref, wd_ref, nw_ref, nmod_ref,
                *out_refs, nlat, seq_lat, seq_ctx, final):
    b, i = pl.program_id(0), pl.program_id(1)
    tm = h_ref.shape[1]
    FF = wd_ref.shape[0]
    nch = FF // FFN_CHUNK
    seq_len, is_ctx = _slab_seq(b, nlat, seq_lat, seq_ctx)
    he = _rows_with_halo(h_ref, hp_ref, hn_ref, i * tm, seq_len)

    def up(c):
        g = slice(c * FFN_CHUNK, (c + 1) * FFN_CHUNK)
        v = slice(FF + c * FFN_CHUNK, FF + (c + 1) * FFN_CHUNK)
        return (jnp.dot(he, wup_ref[:, g], preferred_element_type=F32),
                jnp.dot(he, wup_ref[:, v], preferred_element_type=F32))

    def act(c, ug, uv):
        g = slice(c * FFN_CHUNK, (c + 1) * FFN_CHUNK)
        v = slice(FF + c * FFN_CHUNK, FF + (c + 1) * FFN_CHUNK)
        gate = _conv3_ext(ug, cw_ref[:, g], seq_ctx, is_ctx) + cb_ref[:, g]
        val = _conv3_ext(uv, cw_ref[:, v], seq_ctx, is_ctx) + cb_ref[:, v]
        return (_silu(gate) * val).astype(BF16)

    ups = [up(c) for c in range(min(FFN_LOOKAHEAD, nch))]
    acc = None
    for c in range(nch):
        if c + FFN_LOOKAHEAD < nch:
            ups.append(up(c + FFN_LOOKAHEAD))
        part = jnp.dot(act(c, *ups[c]), wd_ref[c * FFN_CHUNK:(c + 1) * FFN_CHUNK, :], preferred_element_type=F32)
        ups[c] = None
        acc = part if acc is None else acc + part

    m = mod_ref[0]
    x2 = x_ref[0] + m[5:6] * acc
    if final:
        out_refs[0][0] = _rms(x2, nw_ref[...])
    else:
        nm = nmod_ref[0]
        out_refs[0][0] = x2
        out_refs[1][0] = (_rms(x2, nw_ref[...]) * (1.0 + nm[1:2]) + nm[0:1]).astype(BF16)


def _ffn(h2, x1, mod9, w_up, conv_w, conv_b, w_down, next_nw, next_mod9, nslab, nlat, seq_ctx, final):
    S9, N, D = x1.shape
    FF = w_down.shape[0]
    tm = min(512, N)
    row = lambda: pl.BlockSpec((1, tm, D), lambda b, i: (b, i, 0))
    modspec = pl.BlockSpec((1, 6, D), lambda b, i: (b, 0, 0))
    resident = lambda a: pl.BlockSpec(a.shape, lambda b, i: (0,) * a.ndim, pipeline_mode=pl.Buffered(1))
    cb = conv_b.reshape(1, 2 * FF)
    nw = next_nw.reshape(1, D)
    if final:
        out_specs = [row()]
        out_shape = [jax.ShapeDtypeStruct((nslab, N, D), F32)]
    else:
        out_specs = [row(), row()]
        out_shape = [jax.ShapeDtypeStruct((S9, N, D), F32), jax.ShapeDtypeStruct((S9, N, D), BF16)]
    return pl.pallas_call(
        functools.partial(_ffn_kernel, nlat=nlat, seq_lat=N, seq_ctx=seq_ctx, final=final),
        grid=(nslab, N // tm),
        in_specs=_halo_specs(tm, N, D) + [
            row(), modspec, resident(w_up), resident(conv_w), resident(cb), resident(w_down),
            pl.BlockSpec((1, D), lambda b, i: (0, 0)), modspec],
        out_specs=out_specs,
        out_shape=out_shape,
        compiler_params=_cparams(("parallel", "parallel")),
        name="ffn_final" if final else "ffn",
    )(h2, h2, h2, x1, mod9, w_up, conv_w, cb, w_down, nw, next_mod9)


def _axial_angles(n, d):
    rows = n // GRID_W
    r = jnp.repeat(jnp.arange(rows, dtype=F32), GRID_W)
    col = jnp.tile(jnp.arange(GRID_W, dtype=F32), rows)
    quarter = d // 4
    inv = ROPE_BASE ** (-jnp.arange(quarter, dtype=F32) / quarter)
    return jnp.concatenate([r[:, None] * inv, col[:, None] * inv], axis=-1)


def _rope_tables(n):
    a_r = _axial_angles(n, HEAD_DIM)
    cr, sr = jnp.cos(a_r), jnp.sin(a_r)
    cos_ret = jnp.stack([jnp.concatenate([cr, cr], -1), jnp.ones((n, HEAD_DIM), F32)])
    sin_ret = jnp.stack([jnp.concatenate([-sr, sr], -1), jnp.zeros((n, HEAD_DIM), F32)])
    a_m = _axial_angles(n, MLA_ROPE)
    cm, sm = jnp.cos(a_m), jnp.sin(a_m)
    cos_k = jnp.concatenate([cm, cm], -1)
    sin_k = jnp.concatenate([sm, sm], -1)
    pad = MLA_HEAD_PAD - MLA_NOPE - MLA_ROPE
    cos_q = jnp.concatenate([jnp.ones((n, MLA_NOPE), F32), cos_k, jnp.zeros((n, pad), F32)], -1)
    sin_q = jnp.concatenate([jnp.zeros((n, MLA_NOPE), F32), sin_k, jnp.zeros((n, pad), F32)], -1)
    ident_q = jnp.concatenate([jnp.ones((n, MLA_NOPE + MLA_ROPE), F32), jnp.zeros((n, pad), F32)], -1)
    return (cos_ret, sin_ret,
            jnp.stack([cos_q, ident_q]), jnp.stack([sin_q, jnp.zeros_like(sin_q)]),
            jnp.stack([cos_k, jnp.ones_like(cos_k)]), jnp.stack([sin_k, jnp.zeros_like(sin_k)]))


def _rot_half_cols(w):
    half = w.shape[-1] // 2
    return jnp.concatenate([-w[..., half:], w[..., :half]], axis=-1)


def kernel(x, c, ctx, c_ctx, ada_w, ada_b, norm1_w, w_in, gdn_conv_w, gdn_A_log, gdn_dt_bias, gdn_norm_w, mla_q_norm, mla_w_uq, mla_kv_norm, mla_w_ukv, ret_norm_w, w_br_gdn, w_br_mla, w_br_ret, w_out, norm2_w, ffn_w_up, ffn_conv_w, ffn_conv_b, ffn_w_down, final_norm_w):
    B, N, D = x.shape
    M = ctx.shape[1]
    L = ada_w.shape[0]
    W = HEADS * HEAD_DIM
    assert B * M == N and N % SEQ_BLOCK == 0 and M % SEQ_BLOCK == 0
    assert N & (N - 1) == 0 and M & (M - 1) == 0 and D == 1024

    x9 = jnp.concatenate([x, ctx.reshape(1, N, D)], axis=0)
    cc = jnp.zeros((16, D), F32).at[:B].set(c).at[B].set(c_ctx)
    mod = _modulation(cc, ada_w, ada_b)[:, :B + 1].reshape(L, B + 1, 6, D)
    cos_ret, sin_ret, cos_q, sin_q, cos_k, sin_k = _rope_tables(N)

    h = _norm_mod(x9, mod[0], norm1_w[0])
    out = None
    for l in range(L):
        last = l == L - 1
        nslab = B if last else B + 1
        wcol = lambda lo, hi, l=l: lax.slice(w_in, (l, 0, lo), (l + 1, D, hi)).reshape(D, hi - lo)
        o_qkv, o_z, o_ab = 3 * W, 4 * W, 4 * W + 16
        o_cq = o_ab
        o_ckv = o_cq + MLA_Q_RANK
        o_kr = o_ckv + MLA_KV_RANK
        o_ret = o_kr + MLA_ROPE
        o_gate = o_ret + 4 * W
        w_kr = wcol(o_kr, o_ret)
        w_ab = wcol(o_z, o_ab)
        zc = lambda n: jnp.zeros((D, n), F32)
        w_mla = jnp.concatenate([wcol(o_cq, o_kr), w_kr, zc(64), _rot_half_cols(w_kr), zc(64), w_ab, zc(112)],
                                axis=1).astype(BF16)
        assert w_mla.shape[1] == P_AB + 128
        w_gates = jnp.concatenate([wcol(o_gate, w_in.shape[2]), wcol(o_qkv, o_z), wcol(o_ret + 3 * W, o_gate)],
                                  axis=1).astype(BF16)

        qkv = _gdn_proj(h, wcol(0, o_qkv).astype(BF16), gdn_conv_w[l], B, M)
        ret_o = _ret_scan(h, wcol(o_ret, o_ret + 3 * W).astype(BF16), cos_ret, sin_ret, B, M)

        wq = mla_w_uq[l].reshape(MLA_Q_RANK, HEADS, MLA_NOPE + MLA_ROPE)
        zq = jnp.zeros((MLA_Q_RANK, HEADS, MLA_HEAD_PAD - MLA_NOPE - MLA_ROPE), F32)
        wq_p = jnp.concatenate([wq, zq], -1).reshape(MLA_Q_RANK, HEADS * MLA_HEAD_PAD).astype(BF16)
        wq_r = jnp.concatenate([jnp.zeros_like(wq[..., :MLA_NOPE]), _rot_half_cols(wq[..., MLA_NOPE:]), zq],
                               -1).reshape(MLA_Q_RANK, HEADS * MLA_HEAD_PAD).astype(BF16)
        wkv = mla_w_ukv[l].reshape(MLA_KV_RANK, HEADS, 2 * HEAD_DIM)
        wk = wkv[..., :MLA_NOPE].reshape(MLA_KV_RANK, W).astype(BF16)
        wv = wkv[..., MLA_NOPE:].reshape(MLA_KV_RANK, W).astype(BF16)
        Q, K, V, ab, abT = _mla_prep(h, w_mla, w_ab.T.astype(BF16), mla_q_norm[l], wq_p,
                                     wq_r, mla_kv_norm[l], wk, wv, cos_q, sin_q, cos_k, sin_k, B)
        gdn_f, gdn_b = _gdn_scan(qkv, ab, abT, gdn_A_log[l], gdn_dt_bias[l], B, M)
        mla_o = _mla_attn(Q, K, V, B, M, with_ctx=not last)

        x1, h2 = _merge(x9, mod[l], h, gdn_f, gdn_b, ret_o, mla_o, gdn_norm_w[l], ret_norm_w[l], w_gates,
                        w_br_gdn[l].astype(BF16), w_br_mla[l].astype(BF16), w_br_ret[l].astype(BF16),
                        w_out[l].astype(BF16), norm2_w[l], nslab)
        ffn_args = (h2, x1, mod[l], ffn_w_up[l].astype(BF16), ffn_conv_w[l], ffn_conv_b[l],
                    ffn_w_down[l].astype(BF16))
        if last:
            (out,) = _ffn(*ffn_args, final_norm_w, mod[l], nslab, B, M, True)
        else:
            x9, h = _ffn(*ffn_args, norm1_w[l + 1], mod[l + 1], nslab, B, M, False)
    return out
```
